```python
import math, functools
import jax, jax.numpy as jnp
from jax import lax
import numpy as np

D_MODEL = 1024
BATCH = 8
SEQ = 8192
DEPTH = 1
DEC_BATCH = 8
DEC_SEQ = 16
PAST_LEN = 1024

CHUNK = 64
N_HEADS_A = 8
HEAD_DIM_A = 64
V_DIM_A = 2 * HEAD_DIM_A
Q_WIDTH = N_HEADS_A * 2 * HEAD_DIM_A
ATTN_WIDTH = N_HEADS_A * V_DIM_A
GMLP_CHUNK = 128
GMLP_GROUPS = 8
GMLP_WIDTH = 1024
GMLP_GROUP_DIM = GMLP_WIDTH // GMLP_GROUPS
IN_WIDTH = 2 * Q_WIDTH + ATTN_WIDTH + 2 * GMLP_WIDTH
D_FF = 2816
Q_BLOCK = 128
N_MODS = 9
EPS = 1e-6

kernel_name = "streaming_diffattn_gmlp_macaron_adaln"


def lambda_init_of(layer_idx):
    return 0.8 - 0.6 * math.exp(-0.3 * layer_idx)


def rms_norm(x, g):
    xf = x.astype(jnp.float32)
    y = xf * lax.rsqrt(jnp.mean(xf * xf, axis=-1, keepdims=True) + EPS)
    return (y * g.astype(jnp.float32)).astype(x.dtype)


def swiglu(h, w_gu, w_down):
    gu = h @ w_gu
    return (jax.nn.silu(gu[..., :D_FF]) * gu[..., D_FF:]) @ w_down


def alibi_slopes():
    return jnp.asarray(2.0 ** (-8.0 * np.arange(1, N_HEADS_A + 1) / N_HEADS_A), dtype=jnp.float32)


def alibi_chunk_bias(q_pos, k_pos):
    dist = jnp.abs(q_pos[:, None] - k_pos[None, :]).astype(jnp.float32)
    visible = (k_pos[None, :] // CHUNK) <= (q_pos[:, None] // CHUNK)
    bias = -alibi_slopes()[:, None, None] * dist[None]
    return jnp.where(visible[None], bias, -jnp.inf)


def diff_attn_core(q, k, v, q_pos, k_pos, lam):
    s = jnp.einsum('bqhmd,bkhmd->bhmqk', q, k).astype(jnp.float32) * (HEAD_DIM_A ** -0.5)
    s = s + alibi_chunk_bias(q_pos, k_pos)[None, :, None]
    p = jax.nn.softmax(s, axis=-1)
    a = p[:, :, 0] - lam * p[:, :, 1]
    return jnp.einsum('bhqk,bkhe->bqhe', a.astype(v.dtype), v)


def attend_prompt(q, k, v, lam):
    b, t = q.shape[0], q.shape[1]
    nb = t // Q_BLOCK
    qb = jnp.moveaxis(q.reshape(b, nb, Q_BLOCK, N_HEADS_A, 2, HEAD_DIM_A), 1, 0)
    pos = jnp.arange(t, dtype=jnp.int32)
    posb = pos.reshape(nb, Q_BLOCK)

    def one_block(args):
        qi, pi = args
        return diff_attn_core(qi, k, v, pi, pos, lam)

    o = lax.map(one_block, (qb, posb))
    return jnp.moveaxis(o, 0, 1).reshape(b, t, N_HEADS_A, V_DIM_A)


def attend_sample(q, k, v, lam, ck, cv):
    past = ck.shape[1]
    t = q.shape[1]
    k_all = jnp.concatenate([ck.astype(k.dtype), k], axis=1)
    v_all = jnp.concatenate([cv.astype(v.dtype), v], axis=1)
    k_pos = jnp.arange(past + t, dtype=jnp.int32)
    q_pos = past + jnp.arange(t, dtype=jnp.int32)
    return diff_attn_core(q, k_all, v_all, q_pos, k_pos, lam)


def spatial_gate(u, gv, w_s, b_s):
    b, t, _ = u.shape
    rows = min(t, GMLP_CHUNK)
    n = t // rows
    mask = jnp.tril(jnp.ones((rows, rows), dtype=bool))
    ws = jnp.where(mask[None], w_s[:, :rows, :rows], 0.0).astype(gv.dtype)
    gvr = gv.reshape(b, n, rows, GMLP_GROUPS, GMLP_GROUP_DIM)
    mix = jnp.einsum('gts,bnsgc->bntgc', ws, gvr) + b_s[:, :rows].T[None, None, :, :, None]
    return u * mix.reshape(b, t, GMLP_WIDTH)


def layer_apply(x, c, attend, lambda_init, ada_w, ada_b, norm_g, ffn1_wgu, ffn1_wd, w_in,
                q_norm_g, k_norm_g, lambda_qk, attn_subln_g, gmlp_vnorm_g, gmlp_ws, gmlp_bs,
                w_gate, b_gate, w_branch, w_out, ffn2_wgu, ffn2_wd):
    b, t, _ = x.shape
    mods = (jax.nn.silu(c) @ ada_w + ada_b).reshape(b, N_MODS, D_MODEL)[:, :, None, :]
    sh1, sc1, gt1, sh2, sc2, gt2, sh3, sc3, gt3 = [mods[:, i] for i in range(N_MODS)]
    h = rms_norm(x, norm_g[0]) * (1 + sc1) + sh1
    x = x + 0.5 * gt1 * swiglu(h, ffn1_wgu, ffn1_wd)
    h = rms_norm(x, norm_g[1]) * (1 + sc2) + sh2
    z = h @ w_in
    q = rms_norm(z[..., :Q_WIDTH].reshape(b, t, N_HEADS_A, 2, HEAD_DIM_A), q_norm_g)
    k = rms_norm(z[..., Q_WIDTH:2 * Q_WIDTH].reshape(b, t, N_HEADS_A, 2, HEAD_DIM_A), k_norm_g)
    v = z[..., 2 * Q_WIDTH:2 * Q_WIDTH + ATTN_WIDTH].reshape(b, t, N_HEADS_A, V_DIM_A)
    lq = lambda_qk.astype(jnp.float32)
    lam = jnp.exp(jnp.sum(lq[0] * lq[1])) - jnp.exp(jnp.sum(lq[2] * lq[3])) + lambda_init
    o = attend(q, k, v, lam)
    o = (rms_norm(o, attn_subln_g) * (1.0 - lambda_init)).reshape(b, t, ATTN_WIDTH)
    gz = jax.nn.gelu(z[..., 2 * Q_WIDTH + ATTN_WIDTH:])
    u = gz[..., :GMLP_WIDTH]
    gv = rms_norm(gz[..., GMLP_WIDTH:], gmlp_vnorm_g)
    s = spatial_gate(u, gv, gmlp_ws, gmlp_bs)
    gates = jax.nn.sigmoid(h @ w_gate + b_gate)
    mixed = (gates[..., :D_MODEL] * (o @ w_branch[:ATTN_WIDTH])
             + gates[..., D_MODEL:] * (s @ w_branch[ATTN_WIDTH:]))
    x = x + gt2 * (mixed @ w_out)
    h = rms_norm(x, norm_g[2]) * (1 + sc3) + sh3
    x = x + 0.5 * gt3 * swiglu(h, ffn2_wgu, ffn2_wd)
    return rms_norm(x, norm_g[3]), k, v, gv


def setup_inputs(seed: int = 0) -> dict:
    key = jax.random.key(seed)
    ks = jax.random.split(key, 32)
    f32 = jnp.float32
    nrm = lambda k, shape, s: jax.random.normal(k, shape, f32) * s
    L = DEPTH
    return {
        "x_prompt": nrm(ks[0], (BATCH, SEQ, D_MODEL), 1.0),
        "x_sample": nrm(ks[1], (DEC_BATCH, DEC_SEQ, D_MODEL), 1.0),
        "cache_k": nrm(ks[2], (L, DEC_BATCH, PAST_LEN, N_HEADS_A, 2, HEAD_DIM_A), 1.0),
        "cache_v": nrm(ks[3], (L, DEC_BATCH, PAST_LEN, N_HEADS_A, V_DIM_A), 1.0),
        "c_prompt": nrm(ks[4], (BATCH, D_MODEL), 1.0),
        "c_sample": nrm(ks[5], (DEC_BATCH, D_MODEL), 1.0),
        "ada_w": nrm(ks[6], (L, D_MODEL, N_MODS * D_MODEL), 0.5 * D_MODEL ** -0.5),
        "ada_b": nrm(ks[7], (L, N_MODS * D_MODEL), 0.02),
        "norm_g": 1.0 + nrm(ks[8], (L, 4, D_MODEL), 0.02),
        "ffn1_wgu": nrm(ks[9], (L, D_MODEL, 2 * D_FF), D_MODEL ** -0.5),
        "ffn1_wd": nrm(ks[10], (L, D_FF, D_MODEL), D_FF ** -0.5),
        "w_in": nrm(ks[11], (L, D_MODEL, IN_WIDTH), D_MODEL ** -0.5),
        "q_norm_g": 1.0 + nrm(ks[12], (L, HEAD_DIM_A), 0.02),
        "k_norm_g": 1.0 + nrm(ks[13], (L, HEAD_DIM_A), 0.02),
        "lambda_qk": nrm(ks[14], (L, 4, HEAD_DIM_A), 0.1),
        "attn_subln_g": 1.0 + nrm(ks[15], (L, V_DIM_A), 0.02),
        "gmlp_vnorm_g": 1.0 + nrm(ks[16], (L, GMLP_WIDTH), 0.02),
        "gmlp_ws": nrm(ks[17], (L, GMLP_GROUPS, GMLP_CHUNK, GMLP_CHUNK), GMLP_CHUNK ** -0.5),
        "gmlp_bs": 1.0 + nrm(ks[18], (L, GMLP_GROUPS, GMLP_CHUNK), 0.02),
        "w_gate": nrm(ks[19], (L, D_MODEL, 2 * D_MODEL), D_MODEL ** -0.5),
        "b_gate": nrm(ks[20], (L, 2 * D_MODEL), 0.02),
        "w_branch": nrm(ks[21], (L, ATTN_WIDTH + GMLP_WIDTH, D_MODEL), ATTN_WIDTH ** -0.5),
        "w_out": nrm(ks[22], (L, D_MODEL, D_MODEL), D_MODEL ** -0.5),
        "ffn2_wgu": nrm(ks[23], (L, D_MODEL, 2 * D_FF), D_MODEL ** -0.5),
        "ffn2_wd": nrm(ks[24], (L, D_FF, D_MODEL), D_FF ** -0.5),
    }


def reference(x_prompt, x_sample, cache_k, cache_v, c_prompt, c_sample, ada_w, ada_b, norm_g,
              ffn1_wgu, ffn1_wd, w_in, q_norm_g, k_norm_g, lambda_qk, attn_subln_g, gmlp_vnorm_g,
              gmlp_ws, gmlp_bs, w_gate, b_gate, w_branch, w_out, ffn2_wgu, ffn2_wd):
    xp, xs = x_prompt, x_sample
    kp_list, vp_list, ks_list, vs_list, gs_list = [], [], [], [], []
    for l in range(DEPTH):
        w = (ada_w[l], ada_b[l], norm_g[l], ffn1_wgu[l], ffn1_wd[l], w_in[l], q_norm_g[l],
             k_norm_g[l], lambda_qk[l], attn_subln_g[l], gmlp_vnorm_g[l], gmlp_ws[l], gmlp_bs[l],
             w_gate[l], b_gate[l], w_branch[l], w_out[l], ffn2_wgu[l], ffn2_wd[l])
        lam0 = lambda_init_of(l)
        xp, kp, vp, _ = layer_apply(xp, c_prompt, attend_prompt, lam0, *w)
        xs, kn, vn, gvn = layer_apply(
            xs, c_sample, functools.partial(attend_sample, ck=cache_k[l], cv=cache_v[l]), lam0, *w)
        kp_list.append(kp)
        vp_list.append(vp)
        ks_list.append(kn)
        vs_list.append(vn)
        gs_list.append(gvn)
    k_prompt = jnp.stack(kp_list)
    v_prompt = jnp.stack(vp_list)
    k_sample = jnp.stack(ks_list)
    v_sample = jnp.stack(vs_list)
    gmlp_v_sample = jnp.stack(gs_list)
    return (xp, xs, k_prompt, v_prompt, k_sample, v_sample, gmlp_v_sample)
```

```python
import functools
import math

import numpy as np
import jax
import jax.numpy as jnp
from jax import lax
from jax.experimental import pallas as pl
from jax.experimental.pallas import tpu as pltpu

F32 = jnp.float32
BF16 = jnp.bfloat16

EPS = 1e-6
CHUNK = 64
GMLP_CHUNK = 128
GMLP_GROUPS = 8
N_MODS = 9
NEG_BIG = -1e30

V7X_VMEM_BYTES = 64 * 1024 * 1024
VMEM_LIMIT_BYTES = V7X_VMEM_BYTES - 8 * 1024 * 1024
LANES = 128
ALIBI_SPLIT = 256


def _lambda_init(layer_idx):
    return 0.8 - 0.6 * math.exp(-0.3 * layer_idx)


def _const_spec(shape):
    nd = len(shape)
    return pl.BlockSpec(shape, lambda *_: (0,) * nd, pipeline_mode=pl.Buffered(1))


def _params(n_axes):
    return pltpu.CompilerParams(
        dimension_semantics=("arbitrary",) * n_axes, vmem_limit_bytes=VMEM_LIMIT_BYTES)


def _mods_kernel(c_ref, w_ref, b_ref, o_ref):
    c = c_ref[...]
    h = c * jax.nn.sigmoid(c)
    o_ref[...] = jnp.dot(h, w_ref[...], preferred_element_type=F32,
                         precision=lax.Precision.HIGHEST) + b_ref[...]


def _mods(c_all, ada_w, ada_b):
    n, d = c_all.shape
    width = ada_w.shape[1]
    tn = d
    return pl.pallas_call(
        _mods_kernel,
        grid=(width // tn,),
        in_specs=[pl.BlockSpec((n, d), lambda j: (0, 0)),
                  pl.BlockSpec((d, tn), lambda j: (0, j)),
                  pl.BlockSpec((1, tn), lambda j: (0, j))],
        out_specs=pl.BlockSpec((n, tn), lambda j: (0, j)),
        out_shape=jax.ShapeDtypeStruct((n, width), F32),
        compiler_params=_params(1),
        name="adaln_mods",
    )(c_all, ada_w, ada_b.reshape(1, width))


def _rms(x):
    return x * lax.rsqrt(jnp.mean(x * x, axis=-1, keepdims=True) + EPS)


def _mod_norm(x, g, mods_ref, first):
    sh = mods_ref[0, first:first + 1, :]
    sc = mods_ref[0, first + 1:first + 2, :]
    return (_rms(x) * g) * (1.0 + sc) + sh


def _swiglu(h, wg_ref, wu_ref, wd_ref):
    acc = None
    for c in range(wg_ref.shape[0]):
        gate = jnp.dot(h, wg_ref[c], preferred_element_type=F32)
        up = jnp.dot(h, wu_ref[c], preferred_element_type=F32)
        a = (gate * jax.nn.sigmoid(gate) * up).astype(BF16)
        part = jnp.dot(a, wd_ref[c], preferred_element_type=F32)
        acc = part if acc is None else acc + part
    return acc


def _gelu_tanh(x):
    return 0.5 * x * (1.0 + jnp.tanh(math.sqrt(2.0 / math.pi) * (x + 0.044715 * (x * x * x))))


def _chunk_ffn_weights(w_gu, w_d, cs):
    d, f2 = w_gu.shape
    f = f2 // 2
    n = f // cs
    wg = w_gu[:, :f].reshape(d, n, cs).transpose(1, 0, 2).astype(BF16)
    wu = w_gu[:, f:].reshape(d, n, cs).transpose(1, 0, 2).astype(BF16)
    wd = w_d.reshape(n, cs, d).astype(BF16)
    return wg, wu, wd


def _ffn_kernel(x_ref, mods_ref, ng_ref, wg_ref, wu_ref, wd_ref, o_ref):
    x = x_ref[0]
    h = _mod_norm(x, ng_ref[0:1, :], mods_ref, 0).astype(BF16)
    gt = mods_ref[0, 2:3, :]
    o_ref[0] = x + 0.5 * gt * _swiglu(h, wg_ref, wu_ref, wd_ref)


def _ffn1(x, mods, mods_off, norm_g, wg, wu, wd, tm):
    b, t, d = x.shape
    return pl.pallas_call(
        _ffn_kernel,
        grid=(b, t // tm),
        in_specs=[pl.BlockSpec((1, tm, d), lambda bi, i: (bi, i, 0)),
                  pl.BlockSpec((1, N_MODS, d), lambda bi, i: (bi + mods_off, 0, 0)),
                  _const_spec(norm_g.shape), _const_spec(wg.shape), _const_spec(wu.shape),
                  _const_spec(wd.shape)],
        out_specs=pl.BlockSpec((1, tm, d), lambda bi, i: (bi, i, 0)),
        out_shape=jax.ShapeDtypeStruct((b, t, d), F32),
        compiler_params=_params(2),
        name="ffn1",
    )(x, mods, norm_g, wg, wu, wd)


def _proj_kernel(x_ref, mods_ref, ng_ref, win_ref, wgate_ref, bgate_ref, qg_ref, kg_ref, gmat_ref,
                 vng_ref, ws_ref, bs_ref, wbb_ref,
                 q_o, k_o, v_o, kb_o, vb_o, ga_o, gb_o, *maybe_gv_o, rows, q_scale):
    x = x_ref[0]
    tm, d = x.shape
    h = _mod_norm(x, ng_ref[1:2, :], mods_ref, 3).astype(BF16)

    def group_norm(z, g):
        ms = jnp.dot((z * z).astype(BF16), gmat_ref[...], preferred_element_type=F32)
        return z * lax.rsqrt(ms + EPS) * g

    q = group_norm(jnp.dot(h, win_ref[0], preferred_element_type=F32), qg_ref[...])
    q_o[0] = (q * q_scale).astype(BF16)
    k = group_norm(jnp.dot(h, win_ref[1], preferred_element_type=F32), kg_ref[...])
    k_o[0] = k
    kb_o[0] = k.astype(BF16)
    v = jnp.dot(h, win_ref[2], preferred_element_type=F32)
    v_o[0] = v
    vb_o[0] = v.astype(BF16)

    u = _gelu_tanh(jnp.dot(h, win_ref[3], preferred_element_type=F32))
    gz = _gelu_tanh(jnp.dot(h, win_ref[4], preferred_element_type=F32))
    gv = _rms(gz) * vng_ref[...]
    if maybe_gv_o:
        maybe_gv_o[0][0] = gv
    gvb = gv.astype(BF16)

    row_i = lax.broadcasted_iota(jnp.int32, (rows, rows), 0)
    col_i = lax.broadcasted_iota(jnp.int32, (rows, rows), 1)
    tril = col_i <= row_i
    gw = d // GMLP_GROUPS
    ws = [jnp.where(tril, ws_ref[g], 0.0).astype(BF16) for g in range(GMLP_GROUPS)]
    s_chunks = []
    for c in range(tm // rows):
        r0 = c * rows
        mix = jnp.concatenate(
            [jnp.dot(ws[g], gvb[r0:r0 + rows, g * gw:(g + 1) * gw], preferred_element_type=F32)
             for g in range(GMLP_GROUPS)], axis=1)
        s_chunks.append(u[r0:r0 + rows, :] * (mix + bs_ref[...]))
    s = s_chunks[0] if len(s_chunks) == 1 else jnp.concatenate(s_chunks, axis=0)
    sb = jnp.dot(s.astype(BF16), wbb_ref[...], preferred_element_type=F32)

    gate_a = jax.nn.sigmoid(jnp.dot(h, wgate_ref[0], preferred_element_type=F32) + bgate_ref[0:1, :])
    gate_b = jax.nn.sigmoid(jnp.dot(h, wgate_ref[1], preferred_element_type=F32) + bgate_ref[1:2, :])
    ga_o[0] = gate_a.astype(BF16)
    gb_o[0] = (gate_b * sb).astype(BF16)


def _proj(x, mods, mods_off, norm_g, win, wgate, bgate, qg, kg, gmat, vng, ws, bs, wbb, tm, rows,
          q_scale, want_gv):
    b, t, d = x.shape
    row_spec = pl.BlockSpec((1, tm, d), lambda bi, i: (bi, i, 0))
    out_shapes = [jax.ShapeDtypeStruct((b, t, d), BF16),
                  jax.ShapeDtypeStruct((b, t, d), F32),
                  jax.ShapeDtypeStruct((b, t, d), F32),
                  jax.ShapeDtypeStruct((b, t, d), BF16),
                  jax.ShapeDtypeStruct((b, t, d), BF16),
                  jax.ShapeDtypeStruct((b, t, d), BF16),
                  jax.ShapeDtypeStruct((b, t, d), BF16)]
    if want_gv:
        out_shapes.append(jax.ShapeDtypeStruct((b, t, d), F32))
    consts = (norm_g, win, wgate, bgate, qg, kg, gmat, vng, ws, bs, wbb)
    return pl.pallas_call(
        functools.partial(_proj_kernel, rows=rows, q_scale=q_scale),
        grid=(b, t // tm),
        in_specs=[row_spec, pl.BlockSpec((1, N_MODS, d), lambda bi, i: (bi + mods_off, 0, 0))]
        + [_const_spec(a.shape) for a in consts],
        out_specs=[row_spec] * len(out_shapes),
        out_shape=out_shapes,
        compiler_params=_params(2),
        name="mixer_proj",
    )(x, mods, *consts)


def _split_maps(q):
    lane = lax.broadcasted_iota(jnp.int32, q.shape, 1)
    half = q.shape[1] // 2
    zero = jnp.zeros_like(q)
    return jnp.where(lane < half, q, zero), jnp.where(lane >= half, q, zero)


def _qk(a, k):
    return lax.dot_general(a, k, (((1,), (1,)), ((), ())), preferred_element_type=F32)


def _lambda(lq_ref, lam_init):
    lq = lq_ref[...]
    t1 = jnp.sum(lq[0:1, :] * lq[1:2, :], axis=-1, keepdims=True)
    t2 = jnp.sum(lq[2:3, :] * lq[3:4, :], axis=-1, keepdims=True)
    return jnp.exp(t1) - jnp.exp(t2) + lam_init


def _finish_heads(acc1, l1, acc2, l2, lam, sg, lam_init):
    o = acc1 / l1 - lam * (acc2 / l2)
    return (_rms(o) * sg) * (1.0 - lam_init)


def _attn_kernel(slope_ref, q_ref, k_ref, v_ref, eq_ref, ek_ref, lq_ref, sg_ref, o_ref,
                 kx_scr, a1_scr, a2_scr, *, lam_init):
    h = pl.program_id(1)
    i = pl.program_id(2)
    tq = q_ref.shape[1]
    t_all = k_ref.shape[1]
    slope = slope_ref[h]

    @pl.when(i == 0)
    def _():
        kx_scr[:, 0:LANES] = k_ref[0]
        for c in range(t_all // tq):
            kx_scr[c * tq:(c + 1) * tq, LANES:2 * LANES] = ek_ref[...]

    q1, q2 = _split_maps(q_ref[0])
    a1_scr[:, 0:LANES] = q1
    a2_scr[:, 0:LANES] = q2
    a1_scr[:, LANES:2 * LANES] = eq_ref[0]
    a2_scr[:, LANES:2 * LANES] = eq_ref[0]

    row0 = pl.multiple_of(i * tq, tq)
    kd = k_ref[0, pl.ds(row0, tq), :]
    vd = v_ref[0, pl.ds(row0, tq), :]
    r = lax.broadcasted_iota(jnp.int32, (tq, tq), 0)
    c = lax.broadcasted_iota(jnp.int32, (tq, tq), 1)
    visible = (c // CHUNK) <= (r // CHUNK)
    bias = jnp.where(visible, -slope * jnp.abs(r - c).astype(F32), NEG_BIG)

    def first(a):
        s = _qk(a, kd) + bias
        m = jnp.max(s, axis=-1, keepdims=True)
        p = jnp.exp(s - m)
        l = jnp.sum(p, axis=-1, keepdims=True)
        acc = jnp.dot(p.astype(BF16), vd, preferred_element_type=F32)
        return m, l, acc

    init = first(q1) + first(q2)

    def update(state, s, shift, vt):
        m_old, l_old, acc_old = state
        m_new = jnp.maximum(m_old, jnp.max(s, axis=-1, keepdims=True) + shift)
        alpha = jnp.exp(m_old - m_new)
        p = jnp.exp(s - (m_new - shift))
        l_new = alpha * l_old + jnp.sum(p, axis=-1, keepdims=True)
        acc_new = alpha * acc_old + jnp.dot(p.astype(BF16), vt, preferred_element_type=F32)
        return m_new, l_new, acc_new

    def step(j, carry):
        off = pl.multiple_of(j * tq, tq)
        kx = kx_scr[pl.ds(off, tq), :]
        vt = v_ref[0, pl.ds(off, tq), :]
        shift = -slope * ((i - j) * tq).astype(F32)
        s1 = _qk(a1_scr[...], kx)
        s2 = _qk(a2_scr[...], kx)
        return update(carry[0:3], s1, shift, vt) + update(carry[3:6], s2, shift, vt)

    m1, l1, acc1, m2, l2, acc2 = lax.fori_loop(0, i, step, init)
    lam = _lambda(lq_ref, lam_init)
    o_ref[0] = _finish_heads(acc1, l1, acc2, l2, lam, sg_ref[...], lam_init).astype(o_ref.dtype)


def _alibi_columns(slopes, tq):
    n_heads = slopes.shape[0]
    pos = np.arange(tq)
    hi = (pos // ALIBI_SPLIT) * ALIBI_SPLIT
    lo = pos % ALIBI_SPLIT
    eq = np.zeros((n_heads, tq, LANES), np.float32)
    eq[:, :, 0] = -slopes[:, None] * lo[None, :]
    eq[:, :, 1] = -slopes[:, None] * hi[None, :]
    eq[:, :, 2] = slopes[:, None]
    eq[:, :, 3] = slopes[:, None]
    ek = np.zeros((tq, LANES), np.float32)
    ek[:, 0] = 1.0
    ek[:, 1] = 1.0
    ek[:, 2] = lo
    ek[:, 3] = hi
    return jnp.asarray(eq, BF16), jnp.asarray(ek, BF16)


def _attention(q, kb, vb, slopes, lq, sg, lam_init, tq):
    b, t, width = q.shape
    n_heads = width // LANES
    eq, ek = _alibi_columns(slopes, tq)
    grid_spec = pltpu.PrefetchScalarGridSpec(
        num_scalar_prefetch=1,
        grid=(b, n_heads, t // tq),
        in_specs=[pl.BlockSpec((1, tq, LANES), lambda bi, h, i, s: (bi, i, h)),
                  pl.BlockSpec((1, t, LANES), lambda bi, h, i, s: (bi, 0, h)),
                  pl.BlockSpec((1, t, LANES), lambda bi, h, i, s: (bi, 0, h)),
                  pl.BlockSpec((1, tq, LANES), lambda bi, h, i, s: (h, 0, 0)),
                  pl.BlockSpec((tq, LANES), lambda bi, h, i, s: (0, 0)),
                  pl.BlockSpec(lq.shape, lambda bi, h, i, s: (0, 0)),
                  pl.BlockSpec(sg.shape, lambda bi, h, i, s: (0, 0))],
        out_specs=pl.BlockSpec((1, tq, LANES), lambda bi, h, i, s: (bi, i, h)),
        scratch_shapes=[pltpu.VMEM((t, 2 * LANES), BF16),
                        pltpu.VMEM((tq, 2 * LANES), BF16),
                        pltpu.VMEM((tq, 2 * LANES), BF16)],
    )
    return pl.pallas_call(
        functools.partial(_attn_kernel, lam_init=lam_init),
        grid_spec=grid_spec,
        out_shape=jax.ShapeDtypeStruct((b, t, width), BF16),
        compiler_params=_params(3),
        name="diff_attention",
    )(jnp.asarray(slopes, F32), q, kb, vb, eq, ek, lq, sg)


def _attn_cache_kernel(slope_ref, q_ref, kn_ref, vn_ref, ck_ref, cv_ref, lq_ref, sg_ref, o_ref, *,
                       lam_init):
    h = pl.program_id(1)
    slope = slope_ref[h]
    tn = q_ref.shape[1]
    past = ck_ref.shape[1]
    q1, q2 = _split_maps(q_ref[0])
    ck = ck_ref[0].astype(BF16)
    cv = cv_ref[0].astype(BF16)
    kn = kn_ref[0]
    vn = vn_ref[0]

    def bias_for(k0, width):
        qpos = past + lax.broadcasted_iota(jnp.int32, (tn, width), 0)
        kpos = k0 + lax.broadcasted_iota(jnp.int32, (tn, width), 1)
        visible = (kpos // CHUNK) <= (qpos // CHUNK)
        return jnp.where(visible, -slope * jnp.abs(qpos - kpos).astype(F32), NEG_BIG)

    bias_c = bias_for(0, past)
    bias_n = bias_for(past, tn)

    def one_map(a):
        sc = _qk(a, ck) + bias_c
        sn = _qk(a, kn) + bias_n
        m = jnp.maximum(jnp.max(sc, axis=-1, keepdims=True), jnp.max(sn, axis=-1, keepdims=True))
        pc = jnp.exp(sc - m)
        pn = jnp.exp(sn - m)
        l = jnp.sum(pc, axis=-1, keepdims=True) + jnp.sum(pn, axis=-1, keepdims=True)
        acc = (jnp.dot(pc.astype(BF16), cv, preferred_element_type=F32)
               + jnp.dot(pn.astype(BF16), vn, preferred_element_type=F32))
        return l, acc

    l1, acc1 = one_map(q1)
    l2, acc2 = one_map(q2)
    lam = _lambda(lq_ref, lam_init)
    o_ref[0] = _finish_heads(acc1, l1, acc2, l2, lam, sg_ref[...], lam_init).astype(o_ref.dtype)


def _attention_cached(q, kb, vb, cache_k, cache_v, slopes, lq, sg, lam_init):
    b, tn, width = q.shape
    past = cache_k.shape[1]
    n_heads = width // LANES
    new_spec = pl.BlockSpec((1, tn, LANES), lambda bi, h, s: (bi, 0, h))
    past_spec = pl.BlockSpec((1, past, LANES), lambda bi, h, s: (bi, 0, h))
    grid_spec = pltpu.PrefetchScalarGridSpec(
        num_scalar_prefetch=1,
        grid=(b, n_heads),
        in_specs=[new_spec, new_spec, new_spec, past_spec, past_spec,
                  pl.BlockSpec(lq.shape, lambda bi, h, s: (0, 0)),
                  pl.BlockSpec(sg.shape, lambda bi, h, s: (0, 0))],
        out_specs=new_spec,
    )
    return pl.pallas_call(
        functools.partial(_attn_cache_kernel, lam_init=lam_init),
        grid_spec=grid_spec,
        out_shape=jax.ShapeDtypeStruct((b, tn, width), BF16),
        compiler_params=_params(2),
        name="diff_attention_cached",
    )(jnp.asarray(slopes, F32), q, kb, vb, cache_k, cache_v, lq, sg)


def _out_kernel(x_ref, o_ref, ga_ref, gb_ref, mods_ref, ng_ref, wba_ref, wout_ref, wg_ref, wu_ref,
                wd_ref, y_ref):
    x = x_ref[0]
    oa = jnp.dot(o_ref[0], wba_ref[...], preferred_element_type=F32)
    mixed = (ga_ref[0].astype(F32) * oa + gb_ref[0].astype(F32)).astype(BF16)
    x = x + mods_ref[0, 5:6, :] * jnp.dot(mixed, wout_ref[...], preferred_element_type=F32)
    h = _mod_norm(x, ng_ref[2:3, :], mods_ref, 6).astype(BF16)
    x = x + 0.5 * mods_ref[0, 8:9, :] * _swiglu(h, wg_ref, wu_ref, wd_ref)
    y_ref[0] = _rms(x) * ng_ref[3:4, :]


def _mix_ffn2(x, o, ga, gb, mods, mods_off, norm_g, wba, wout, wg, wu, wd, tm):
    b, t, d = x.shape
    row_spec = pl.BlockSpec((1, tm, d), lambda bi, i: (bi, i, 0))
    consts = (norm_g, wba, wout, wg, wu, wd)
    return pl.pallas_call(
        _out_kernel,
        grid=(b, t // tm),
        in_specs=[row_spec] * 4 + [pl.BlockSpec((1, N_MODS, d), lambda bi, i: (bi + mods_off, 0, 0))]
        + [_const_spec(a.shape) for a in consts],
        out_specs=row_spec,
        out_shape=jax.ShapeDtypeStruct((b, t, d), F32),
        compiler_params=_params(2),
        name="mix_ffn2",
    )(x, o, ga, gb, mods, *consts)


def _row_tile(t, target):
    tm = min(t, target)
    while t % tm:
        tm //= 2
    return tm


def kernel(x_prompt, x_sample, cache_k, cache_v, c_prompt, c_sample, ada_w, ada_b, norm_g, ffn1_wgu,
           ffn1_wd, w_in, q_norm_g, k_norm_g, lambda_qk, attn_subln_g, gmlp_vnorm_g, gmlp_ws, gmlp_bs,
           w_gate, b_gate, w_branch, w_out, ffn2_wgu, ffn2_wd):
    assert ada_w.shape[0] == 1, "single-layer step"
    bp, tp, d = x_prompt.shape
    bs_, ts, _ = x_sample.shape
    _, _, past, n_heads, _, hd = cache_k.shape
    assert n_heads == 8 and 2 * hd == LANES and n_heads * LANES == d
    lam_init = _lambda_init(0)
    slopes = 2.0 ** (-8.0 * np.arange(1, n_heads + 1) / n_heads)
    q_scale = float(hd) ** -0.5

    ffn_cs = 256
    wg1, wu1, wd1 = _chunk_ffn_weights(ffn1_wgu[0], ffn1_wd[0], ffn_cs)
    wg2, wu2, wd2 = _chunk_ffn_weights(ffn2_wgu[0], ffn2_wd[0], ffn_cs)
    win = w_in[0].reshape(d, 5, d).transpose(1, 0, 2).astype(BF16)
    wgate = w_gate[0].reshape(d, 2, d).transpose(1, 0, 2).astype(BF16)
    bgate = b_gate[0].reshape(2, d)
    wba = w_branch[0, :d].astype(BF16)
    wbb = w_branch[0, d:].astype(BF16)
    wout = w_out[0].astype(BF16)
    ng = norm_g[0]
    qg = jnp.tile(q_norm_g[0], d // hd).reshape(1, d)
    kg = jnp.tile(k_norm_g[0], d // hd).reshape(1, d)
    grp = np.arange(d) // hd
    gmat = jnp.asarray((grp[:, None] == grp[None, :]).astype(np.float32) / hd, BF16)
    vng = gmlp_vnorm_g[0].reshape(1, d)
    sg = attn_subln_g[0].reshape(1, LANES)
    lq = lambda_qk[0]

    mods = _mods(jnp.concatenate([c_prompt, c_sample], axis=0), ada_w[0], ada_b[0])
    mods = mods.reshape(bp + bs_, N_MODS, d)

    def gmlp_consts(t):
        rows = min(t, GMLP_CHUNK)
        ws = gmlp_ws[0][:, :rows, :rows]
        bs = jnp.repeat(gmlp_bs[0][:, :rows].T, d // GMLP_GROUPS, axis=1)
        return rows, ws, bs

    def layer(x, mods_off, attend, tm_ffn, tm_proj, want_gv):
        b, t, _ = x.shape
        rows, ws, bs = gmlp_consts(t)
        x1 = _ffn1(x, mods, mods_off, ng, wg1, wu1, wd1, _row_tile(t, tm_ffn))
        tmp = max(_row_tile(t, tm_proj), rows)
        outs = _proj(x1, mods, mods_off, ng, win, wgate, bgate, qg, kg, gmat, vng, ws, bs, wbb,
                     tmp, rows, q_scale, want_gv)
        q, k, v, kb, vb, ga, gb = outs[:7]
        o = attend(q, kb, vb)
        y = _mix_ffn2(x1, o, ga, gb, mods, mods_off, ng, wba, wout, wg2, wu2, wd2,
                      _row_tile(t, tm_ffn))
        return y, k, v, (outs[7] if want_gv else None)

    tq = _row_tile(tp, 512)
    y_p, k_p, v_p, _ = layer(
        x_prompt, 0, lambda q, kb, vb: _attention(q, kb, vb, slopes, lq, sg, lam_init, tq),
        512, 256, False)

    ck = cache_k[0].reshape(bs_, past, d)
    cv = cache_v[0].reshape(bs_, past, d)
    y_s, k_s, v_s, gv_s = layer(
        x_sample, bp,
        lambda q, kb, vb: _attention_cached(q, kb, vb, ck, cv, slopes, lq, sg, lam_init),
        512, 256, True)

    return (y_p, y_s,
            k_p.reshape(1, bp, tp, n_heads, 2, hd), v_p.reshape(1, bp, tp, n_heads, 2 * hd),
            k_s.reshape(1, bs_, ts, n_heads, 2, hd), v_s.reshape(1, bs_, ts, n_heads, 2 * hd),
            gv_s.reshape(1, bs_, ts, d))
```

```python
import functools
import math

import numpy as np
import jax
import jax.numpy as jnp
from jax import lax
from jax.experimental import pallas as pl
from jax.experimental.pallas import tpu as pltpu

F32 = jnp.float32
BF16 = jnp.bfloat16

EPS = 1e-6
CHUNK = 64
GMLP_CHUNK = 128
GMLP_GROUPS = 8
N_MODS = 9
NEG_BIG = -1e30

V7X_VMEM_BYTES = 64 * 1024 * 1024
VMEM_LIMIT_BYTES = V7X_VMEM_BYTES - 8 * 1024 * 1024
LANES = 128
MXU_DIM = 256
ALIBI_SPLIT = 256
ALIBI_PERIOD = 1024


def _lambda_init(layer_idx):
    return 0.8 - 0.6 * math.exp(-0.3 * layer_idx)


def _const_spec(shape):
    nd = len(shape)
    return pl.BlockSpec(shape, lambda *_: (0,) * nd, pipeline_mode=pl.Buffered(1))


def _params(n_axes):
    return pltpu.CompilerParams(
        dimension_semantics=("arbitrary",) * n_axes, vmem_limit_bytes=VMEM_LIMIT_BYTES)


def _mods_kernel(c_ref, w_ref, b_ref, o_ref):
    c = c_ref[...]
    h = c * jax.nn.sigmoid(c)
    o_ref[...] = jnp.dot(h, w_ref[...], preferred_element_type=F32,
                         precision=lax.Precision.HIGHEST) + b_ref[...]


def _mods(c_all, ada_w, ada_b):
    n, d = c_all.shape
    width = ada_w.shape[1]
    tn = d
    return pl.pallas_call(
        _mods_kernel,
        grid=(width // tn,),
        in_specs=[pl.BlockSpec((n, d), lambda j: (0, 0)),
                  pl.BlockSpec((d, tn), lambda j: (0, j)),
                  pl.BlockSpec((1, tn), lambda j: (0, j))],
        out_specs=pl.BlockSpec((n, tn), lambda j: (0, j)),
        out_shape=jax.ShapeDtypeStruct((n, width), F32),
        compiler_params=_params(1),
        name="adaln_mods",
    )(c_all, ada_w, ada_b.reshape(1, width))


def _rms(x):
    return x * lax.rsqrt(jnp.mean(x * x, axis=-1, keepdims=True) + EPS)


def _mod_norm(x, g, mods_ref, first):
    sh = mods_ref[0, first:first + 1, :]
    sc = mods_ref[0, first + 1:first + 2, :]
    return (_rms(x) * g) * (1.0 + sc) + sh


def _swiglu(h, wg_ref, wu_ref, wd_ref):
    acc = None
    for c in range(wg_ref.shape[0]):
        gate = jnp.dot(h, wg_ref[c], preferred_element_type=F32)
        up = jnp.dot(h, wu_ref[c], preferred_element_type=F32)
        a = (gate * jax.nn.sigmoid(gate) * up).astype(BF16)
        part = jnp.dot(a, wd_ref[c], preferred_element_type=F32)
        acc = part if acc is None else acc + part
    return acc


def _gelu_tanh(x):
    return 0.5 * x * (1.0 + jnp.tanh(math.sqrt(2.0 / math.pi) * (x + 0.044715 * (x * x * x))))


def _chunk_ffn_weights(w_gu, w_d, cs):
    d, f2 = w_gu.shape
    f = f2 // 2
    n = f // cs
    wg = w_gu[:, :f].reshape(d, n, cs).transpose(1, 0, 2).astype(BF16)
    wu = w_gu[:, f:].reshape(d, n, cs).transpose(1, 0, 2).astype(BF16)
    wd = w_d.reshape(n, cs, d).astype(BF16)
    return wg, wu, wd


def _ffn_kernel(x_ref, mods_ref, ng_ref, wg_ref, wu_ref, wd_ref, o_ref):
    x = x_ref[0]
    h = _mod_norm(x, ng_ref[0:1, :], mods_ref, 0).astype(BF16)
    gt = mods_ref[0, 2:3, :]
    o_ref[0] = x + 0.5 * gt * _swiglu(h, wg_ref, wu_ref, wd_ref)


def _ffn1(x, mods, mods_off, norm_g, wg, wu, wd, tm):
    b, t, d = x.shape
    return pl.pallas_call(
        _ffn_kernel,
        grid=(b, t // tm),
        in_specs=[pl.BlockSpec((1, tm, d), lambda bi, i: (bi, i, 0)),
                  pl.BlockSpec((1, N_MODS, d), lambda bi, i: (bi + mods_off, 0, 0)),
                  _const_spec(norm_g.shape), _const_spec(wg.shape), _const_spec(wu.shape),
                  _const_spec(wd.shape)],
        out_specs=pl.BlockSpec((1, tm, d), lambda bi, i: (bi, i, 0)),
        out_shape=jax.ShapeDtypeStruct((b, t, d), F32),
        compiler_params=_params(2),
        name="ffn1",
    )(x, mods, norm_g, wg, wu, wd)


def _proj_kernel(x_ref, mods_ref, ng_ref, win_ref, wgate_ref, bgate_ref, qg_ref, kg_ref, gmat_ref,
                 vng_ref, ws_ref, bs_ref, wbb_ref,
                 q_o, k_o, v_o, kb_o, vb_o, ga_o, gb_o, *maybe_gv_o, rows, q_scale):
    x = x_ref[0]
    tm, d = x.shape
    h = _mod_norm(x, ng_ref[1:2, :], mods_ref, 3).astype(BF16)

    def group_norm(z, g):
        ms = jnp.dot((z * z).astype(BF16), gmat_ref[...], preferred_element_type=F32)
        return z * lax.rsqrt(ms + EPS) * g

    q = group_norm(jnp.dot(h, win_ref[0], preferred_element_type=F32), qg_ref[...])
    q_o[0] = (q * q_scale).astype(BF16)
    k = group_norm(jnp.dot(h, win_ref[1], preferred_element_type=F32), kg_ref[...])
    k_o[0] = k
    kb_o[0] = k.astype(BF16)
    v = jnp.dot(h, win_ref[2], preferred_element_type=F32)
    v_o[0] = v
    vb_o[0] = v.astype(BF16)

    u = _gelu_tanh(jnp.dot(h, win_ref[3], preferred_element_type=F32))
    gz = _gelu_tanh(jnp.dot(h, win_ref[4], preferred_element_type=F32))
    gv = _rms(gz) * vng_ref[...]
    if maybe_gv_o:
        maybe_gv_o[0][0] = gv
    gvb = gv.astype(BF16)

    row_i = lax.broadcasted_iota(jnp.int32, (rows, rows), 0)
    col_i = lax.broadcasted_iota(jnp.int32, (rows, rows), 1)
    tril = col_i <= row_i
    gw = d // GMLP_GROUPS
    ws = [jnp.where(tril, ws_ref[g], 0.0).astype(BF16) for g in range(GMLP_GROUPS)]
    s_chunks = []
    for c in range(tm // rows):
        r0 = c * rows
        mix = jnp.concatenate(
            [jnp.dot(ws[g], gvb[r0:r0 + rows, g * gw:(g + 1) * gw], preferred_element_type=F32)
             for g in range(GMLP_GROUPS)], axis=1)
        s_chunks.append(u[r0:r0 + rows, :] * (mix + bs_ref[...]))
    s = s_chunks[0] if len(s_chunks) == 1 else jnp.concatenate(s_chunks, axis=0)
    sb = jnp.dot(s.astype(BF16), wbb_ref[...], preferred_element_type=F32)

    gate_a = jax.nn.sigmoid(jnp.dot(h, wgate_ref[0], preferred_element_type=F32) + bgate_ref[0:1, :])
    gate_b = jax.nn.sigmoid(jnp.dot(h, wgate_ref[1], preferred_element_type=F32) + bgate_ref[1:2, :])
    ga_o[0] = gate_a.astype(BF16)
    gb_o[0] = (gate_b * sb).astype(BF16)


def _proj(x, mods, mods_off, norm_g, win, wgate, bgate, qg, kg, gmat, vng, ws, bs, wbb, tm, rows,
          q_scale, want_gv):
    b, t, d = x.shape
    row_spec = pl.BlockSpec((1, tm, d), lambda bi, i: (bi, i, 0))
    out_shapes = [jax.ShapeDtypeStruct((b, t, d), BF16),
                  jax.ShapeDtypeStruct((b, t, d), F32),
                  jax.ShapeDtypeStruct((b, t, d), F32),
                  jax.ShapeDtypeStruct((b, t, d), BF16),
                  jax.ShapeDtypeStruct((b, t, d), BF16),
                  jax.ShapeDtypeStruct((b, t, d), BF16),
                  jax.ShapeDtypeStruct((b, t, d), BF16)]
    if want_gv:
        out_shapes.append(jax.ShapeDtypeStruct((b, t, d), F32))
    consts = (norm_g, win, wgate, bgate, qg, kg, gmat, vng, ws, bs, wbb)
    return pl.pallas_call(
        functools.partial(_proj_kernel, rows=rows, q_scale=q_scale),
        grid=(b, t // tm),
        in_specs=[row_spec, pl.BlockSpec((1, N_MODS, d), lambda bi, i: (bi + mods_off, 0, 0))]
        + [_const_spec(a.shape) for a in consts],
        out_specs=[row_spec] * len(out_shapes),
        out_shape=out_shapes,
        compiler_params=_params(2),
        name="mixer_proj",
    )(x, mods, *consts)


def _split_maps(q):
    lane = lax.broadcasted_iota(jnp.int32, q.shape, 1)
    half = q.shape[1] // 2
    zero = jnp.zeros_like(q)
    return jnp.where(lane < half, q, zero), jnp.where(lane >= half, q, zero)


def _qk(a, k):
    return lax.dot_general(a, k, (((1,), (1,)), ((), ())), preferred_element_type=F32)


def _lambda(lq_ref, lam_init):
    lq = lq_ref[...]
    t1 = jnp.sum(lq[0:1, :] * lq[1:2, :], axis=-1, keepdims=True)
    t2 = jnp.sum(lq[2:3, :] * lq[3:4, :], axis=-1, keepdims=True)
    return jnp.exp(t1) - jnp.exp(t2) + lam_init


def _finish_heads(acc1, l1, acc2, l2, lam, sg, lam_init):
    o = acc1 / l1 - lam * (acc2 / l2)
    return (_rms(o) * sg) * (1.0 - lam_init)


def _attn_kernel(slope_ref, q_ref, k_ref, v_ref, eqt_ref, ek_ref, lq_ref, sg_ref, o_ref,
                 kx_scr, vt_scr, bias_scr, a1t_scr, a2t_scr, s_scr, *, lam_init, wide):
    h = pl.program_id(1)
    i = pl.program_id(2)
    tq = q_ref.shape[1]
    t_all = k_ref.shape[1]
    slope = slope_ref[h]
    per_wide = wide // tq
    n_wide = i // per_wide

    @pl.when(i == 0)
    def _():
        kx_scr[:, 0:LANES] = k_ref[0]
        for c in range(t_all // wide):
            kx_scr[c * wide:(c + 1) * wide, LANES:2 * LANES] = ek_ref[0]
            vt_scr[0:LANES, c * wide:(c + 1) * wide] = (
                v_ref[0, c * wide:(c + 1) * wide, :].astype(F32).T.astype(BF16))
        vt_scr[LANES:, :] = jnp.ones((vt_scr.shape[0] - LANES, t_all), BF16)
        s_pos = lax.broadcasted_iota(jnp.int32, (wide, tq), 0)
        for c in range(per_wide):
            t_pos = c * tq + lax.broadcasted_iota(jnp.int32, (wide, tq), 1)
            visible = (s_pos // CHUNK) <= (t_pos // CHUNK)
            fix = (jnp.abs(t_pos - s_pos) - (t_pos - s_pos)).astype(F32)
            bias_scr[c] = jnp.where(visible, -slope * fix, NEG_BIG)

    qt = q_ref[0].astype(F32).T
    row = lax.broadcasted_iota(jnp.int32, qt.shape, 0)
    zero = jnp.zeros_like(qt)
    a1t_scr[0:LANES, :] = jnp.where(row < LANES // 2, qt, zero).astype(BF16)
    a2t_scr[0:LANES, :] = jnp.where(row >= LANES // 2, qt, zero).astype(BF16)
    a1t_scr[LANES:2 * LANES, :] = eqt_ref[0]
    a2t_scr[LANES:2 * LANES, :] = eqt_ref[0]

    def scores_to(slot, j):
        kx = kx_scr[pl.ds(pl.multiple_of(j * wide, wide), wide), :]
        s1 = jnp.dot(kx, a1t_scr[...], preferred_element_type=F32)
        s2 = jnp.dot(kx, a2t_scr[...], preferred_element_type=F32)
        s_scr[slot, 0] = s1
        s_scr[slot, 1] = s2
        return jnp.max(s1, axis=0, keepdims=True), jnp.max(s2, axis=0, keepdims=True)

    def update(m_old, acc_old, s, s_max, shift, vt):
        m_new = jnp.maximum(m_old, s_max + shift)
        alpha = jnp.exp2(m_old - m_new)
        p = jnp.exp2(s - (m_new - shift))
        return m_new, alpha * acc_old + jnp.dot(vt, p.astype(BF16), preferred_element_type=F32)

    def values(j):
        return vt_scr[:, pl.ds(pl.multiple_of(j * wide, wide), wide)]

    def step(slot, j, state):
        m1, acc1, m2, acc2, max1, max2 = state
        next_max = scores_to(1 - slot, j + 1)
        shift = -slope * ((n_wide - j) * wide).astype(F32)
        vt = values(j)
        return (update(m1, acc1, s_scr[slot, 0], max1, shift, vt)
                + update(m2, acc2, s_scr[slot, 1], max2, shift, vt) + next_max)

    odd = n_wide % 2
    m0 = jnp.full((1, tq), NEG_BIG, F32)
    acc0 = jnp.zeros((vt_scr.shape[0], tq), F32)
    state = (m0, acc0, m0, acc0) + scores_to(odd, 0)
    state = lax.cond(odd == 1, lambda st: step(1, 0, st), lambda st: st, state)

    def pair(jj, st):
        j = odd + 2 * jj
        return step(1, j + 1, step(0, j, st))

    m1, acc1, m2, acc2, _, _ = lax.fori_loop(0, n_wide // 2, pair, state)

    bias = bias_scr[i % per_wide]
    vt = values(n_wide)
    s1 = s_scr[0, 0] + bias
    s2 = s_scr[0, 1] + bias
    zero_shift = jnp.float32(0.0)
    _, acc1 = update(m1, acc1, s1, jnp.max(s1, axis=0, keepdims=True), zero_shift, vt)
    _, acc2 = update(m2, acc2, s2, jnp.max(s2, axis=0, keepdims=True), zero_shift, vt)

    lam = _lambda(lq_ref, lam_init)
    ot = acc1[0:LANES] / acc1[LANES:LANES + 1] - lam * (acc2[0:LANES] / acc2[LANES:LANES + 1])
    ot = ot * lax.rsqrt(jnp.mean(ot * ot, axis=0, keepdims=True) + EPS)
    o_ref[0] = ((ot.T * sg_ref[...]) * (1.0 - lam_init)).astype(o_ref.dtype)


def _bf16_pieces(x, n):
    pieces, rest = [], np.asarray(x, np.float32)
    for _ in range(n):
        p = rest.astype(BF16).astype(np.float32)
        pieces.append(p)
        rest = rest - p
    return pieces


def _alibi_rows_cols(slopes, period):
    n_heads = slopes.shape[0]
    pos = np.arange(period)
    offs = ((pos // ALIBI_SPLIT) * ALIBI_SPLIT, pos % ALIBI_SPLIT)
    eqt = np.zeros((n_heads, LANES, period), np.float32)
    ek = np.zeros((n_heads, period, LANES), np.float32)
    lane = 0
    for piece in _bf16_pieces(slopes, 3):
        for off in offs:
            eqt[:, lane, :] = off[None, :]
            ek[:, :, lane] = -piece[:, None]
            eqt[:, lane + 1, :] = piece[:, None]
            ek[:, :, lane + 1] = off[None, :]
            lane += 2
    return jnp.asarray(eqt, BF16), jnp.asarray(ek, BF16)


def _attention(q, kb, vb, slopes, lq, sg, lam_init):
    b, t, width = q.shape
    n_heads = width // LANES
    tq = min(MXU_DIM, t)
    wide = min(ALIBI_PERIOD, t)
    assert t % wide == 0 and wide % tq == 0
    eqt, ek = _alibi_rows_cols(slopes, wide)
    per_wide = wide // tq
    ones_rows = 16
    grid_spec = pltpu.PrefetchScalarGridSpec(
        num_scalar_prefetch=1,
        grid=(b, n_heads, t // tq),
        in_specs=[pl.BlockSpec((1, tq, LANES), lambda bi, h, i, s: (bi, i, h)),
                  pl.BlockSpec((1, t, LANES), lambda bi, h, i, s: (bi, 0, h)),
                  pl.BlockSpec((1, t, LANES), lambda bi, h, i, s: (bi, 0, h)),
                  pl.BlockSpec((1, LANES, tq), lambda bi, h, i, s: (h, 0, i % per_wide)),
                  pl.BlockSpec((1, wide, LANES), lambda bi, h, i, s: (h, 0, 0)),
                  pl.BlockSpec(lq.shape, lambda bi, h, i, s: (0, 0)),
                  pl.BlockSpec(sg.shape, lambda bi, h, i, s: (0, 0))],
        out_specs=pl.BlockSpec((1, tq, LANES), lambda bi, h, i, s: (bi, i, h)),
        scratch_shapes=[pltpu.VMEM((t, 2 * LANES), BF16),
                        pltpu.VMEM((LANES + ones_rows, t), BF16),
                        pltpu.VMEM((per_wide, wide, tq), F32),
                        pltpu.VMEM((2 * LANES, tq), BF16),
                        pltpu.VMEM((2 * LANES, tq), BF16),
                        pltpu.VMEM((2, 2, wide, tq), F32)],
    )
    return pl.pallas_call(
        functools.partial(_attn_kernel, lam_init=lam_init, wide=wide),
        grid_spec=grid_spec,
        out_shape=jax.ShapeDtypeStruct((b, t, width), BF16),
        compiler_params=_params(3),
        name="diff_attention",
    )(jnp.asarray(slopes, F32), q, kb, vb, eqt, ek, lq, sg)


def _attn_cache_kernel(slope_ref, q_ref, kn_ref, vn_ref, ck_ref, cv_ref, lq_ref, sg_ref, o_ref, *,
                       lam_init):
    h = pl.program_id(1)
    slope = slope_ref[h]
    tn = q_ref.shape[1]
    past = ck_ref.shape[1]
    q1, q2 = _split_maps(q_ref[0])
    ck = ck_ref[0].astype(BF16)
    cv = cv_ref[0].astype(BF16)
    kn = kn_ref[0]
    vn = vn_ref[0]

    def bias_for(k0, width):
        qpos = past + lax.broadcasted_iota(jnp.int32, (tn, width), 0)
        kpos = k0 + lax.broadcasted_iota(jnp.int32, (tn, width), 1)
        visible = (kpos // CHUNK) <= (qpos // CHUNK)
        return jnp.where(visible, -slope * jnp.abs(qpos - kpos).astype(F32), NEG_BIG)

    bias_c = bias_for(0, past)
    bias_n = bias_for(past, tn)

    def one_map(a):
        sc = _qk(a, ck) + bias_c
        sn = _qk(a, kn) + bias_n
        m = jnp.maximum(jnp.max(sc, axis=-1, keepdims=True), jnp.max(sn, axis=-1, keepdims=True))
        pc = jnp.exp2(sc - m)
        pn = jnp.exp2(sn - m)
        l = jnp.sum(pc, axis=-1, keepdims=True) + jnp.sum(pn, axis=-1, keepdims=True)
        acc = (jnp.dot(pc.astype(BF16), cv, preferred_element_type=F32)
               + jnp.dot(pn.astype(BF16), vn, preferred_element_type=F32))
        return l, acc

    l1, acc1 = one_map(q1)
    l2, acc2 = one_map(q2)
    lam = _lambda(lq_ref, lam_init)
    o_ref[0] = _finish_heads(acc1, l1, acc2, l2, lam, sg_ref[...], lam_init).astype(o_ref.dtype)


def _attention_cached(q, kb, vb, cache_k, cache_v, slopes, lq, sg, lam_init):
    b, tn, width = q.shape
    past = cache_k.shape[1]
    n_heads = width // LANES
    new_spec = pl.BlockSpec((1, tn, LANES), lambda bi, h, s: (bi, 0, h))
    past_spec = pl.BlockSpec((1, past, LANES), lambda bi, h, s: (bi, 0, h))
    grid_spec = pltpu.PrefetchScalarGridSpec(
        num_scalar_prefetch=1,
        grid=(b, n_heads),
        in_specs=[new_spec, new_spec, new_spec, past_spec, past_spec,
                  pl.BlockSpec(lq.shape, lambda bi, h, s: (0, 0)),
                  pl.BlockSpec(sg.shape, lambda bi, h, s: (0, 0))],
        out_specs=new_spec,
    )
    return pl.pallas_call(
        functools.partial(_attn_cache_kernel, lam_init=lam_init),
        grid_spec=grid_spec,
        out_shape=jax.ShapeDtypeStruct((b, tn, width), BF16),
        compiler_params=_params(2),
        name="diff_attention_cached",
    )(jnp.asarray(slopes, F32), q, kb, vb, cache_k, cache_v, lq, sg)


def _out_kernel(x_ref, o_ref, ga_ref, gb_ref, mods_ref, ng_ref, wba_ref, wout_ref, wg_ref, wu_ref,
                wd_ref, y_ref):
    x = x_ref[0]
    oa = jnp.dot(o_ref[0], wba_ref[...], preferred_element_type=F32)
    mixed = (ga_ref[0].astype(F32) * oa + gb_ref[0].astype(F32)).astype(BF16)
    x = x + mods_ref[0, 5:6, :] * jnp.dot(mixed, wout_ref[...], preferred_element_type=F32)
    h = _mod_norm(x, ng_ref[2:3, :], mods_ref, 6).astype(BF16)
    x = x + 0.5 * mods_ref[0, 8:9, :] * _swiglu(h, wg_ref, wu_ref, wd_ref)
    y_ref[0] = _rms(x) * ng_ref[3:4, :]


def _mix_ffn2(x, o, ga, gb, mods, mods_off, norm_g, wba, wout, wg, wu, wd, tm):
    b, t, d = x.shape
    row_spec = pl.BlockSpec((1, tm, d), lambda bi, i: (bi, i, 0))
    consts = (norm_g, wba, wout, wg, wu, wd)
    return pl.pallas_call(
        _out_kernel,
        grid=(b, t // tm),
        in_specs=[row_spec] * 4 + [pl.BlockSpec((1, N_MODS, d), lambda bi, i: (bi + mods_off, 0, 0))]
        + [_const_spec(a.shape) for a in consts],
        out_specs=row_spec,
        out_shape=jax.ShapeDtypeStruct((b, t, d), F32),
        compiler_params=_params(2),
        name="mix_ffn2",
    )(x, o, ga, gb, mods, *consts)


def _row_tile(t, target):
    tm = min(t, target)
    while t % tm:
        tm //= 2
    return tm


def kernel(x_prompt, x_sample, cache_k, cache_v, c_prompt, c_sample, ada_w, ada_b, norm_g, ffn1_wgu,
           ffn1_wd, w_in, q_norm_g, k_norm_g, lambda_qk, attn_subln_g, gmlp_vnorm_g, gmlp_ws, gmlp_bs,
           w_gate, b_gate, w_branch, w_out, ffn2_wgu, ffn2_wd):
    assert ada_w.shape[0] == 1, "single-layer step"
    bp, tp, d = x_prompt.shape
    bs_, ts, _ = x_sample.shape
    _, _, past, n_heads, _, hd = cache_k.shape
    assert n_heads == 8 and 2 * hd == LANES and n_heads * LANES == d
    lam_init = _lambda_init(0)
    log2e = math.log2(math.e)
    slopes = (2.0 ** (-8.0 * np.arange(1, n_heads + 1) / n_heads) * log2e).astype(np.float32)
    q_scale = float(hd) ** -0.5 * log2e

    ffn_cs = 256
    wg1, wu1, wd1 = _chunk_ffn_weights(ffn1_wgu[0], ffn1_wd[0], ffn_cs)
    wg2, wu2, wd2 = _chunk_ffn_weights(ffn2_wgu[0], ffn2_wd[0], ffn_cs)
    win = w_in[0].reshape(d, 5, d).transpose(1, 0, 2).astype(BF16)
    wgate = w_gate[0].reshape(d, 2, d).transpose(1, 0, 2).astype(BF16)
    bgate = b_gate[0].reshape(2, d)
    wba = w_branch[0, :d].astype(BF16)
    wbb = w_branch[0, d:].astype(BF16)
    wout = w_out[0].astype(BF16)
    ng = norm_g[0]
    qg = jnp.tile(q_norm_g[0], d // hd).reshape(1, d)
    kg = jnp.tile(k_norm_g[0], d // hd).reshape(1, d)
    grp = np.arange(d) // hd
    gmat = jnp.asarray((grp[:, None] == grp[None, :]).astype(np.float32) / hd, BF16)
    vng = gmlp_vnorm_g[0].reshape(1, d)
    sg = attn_subln_g[0].reshape(1, LANES)
    lq = lambda_qk[0]

    mods = _mods(jnp.concatenate([c_prompt, c_sample], axis=0), ada_w[0], ada_b[0])
    mods = mods.reshape(bp + bs_, N_MODS, d)

    def gmlp_consts(t):
        rows = min(t, GMLP_CHUNK)
        ws = gmlp_ws[0][:, :rows, :rows]
        bs = jnp.repeat(gmlp_bs[0][:, :rows].T, d // GMLP_GROUPS, axis=1)
        return rows, ws, bs

    def layer(x, mods_off, attend, tm_ffn, tm_proj, want_gv):
        b, t, _ = x.shape
        rows, ws, bs = gmlp_consts(t)
        x1 = _ffn1(x, mods, mods_off, ng, wg1, wu1, wd1, _row_tile(t, tm_ffn))
        tmp = max(_row_tile(t, tm_proj), rows)
        outs = _proj(x1, mods, mods_off, ng, win, wgate, bgate, qg, kg, gmat, vng, ws, bs, wbb,
                     tmp, rows, q_scale, want_gv)
        q, k, v, kb, vb, ga, gb = outs[:7]
        o = attend(q, kb, vb)
        y = _mix_ffn2(x1, o, ga, gb, mods, mods_off, ng, wba, wout, wg2, wu2, wd2,
                      _row_tile(t, tm_ffn))
        return y, k, v, (outs[7] if want_gv else None)

    y_p, k_p, v_p, _ = layer(
        x_prompt, 0, lambda q, kb, vb: _attention(q, kb, vb, slopes, lq, sg, lam_init),
        512, 256, False)

    ck = cache_k[0].reshape(bs_, past, d)
    cv = cache_v[0].reshape(bs_, past, d)
    y_s, k_s, v_s, gv_s = layer(
        x_sample, bp,
        lambda q, kb, vb: _attention_cached(q, kb, vb, ck, cv, slopes, lq, sg, lam_init),
        512, 256, True)

    return (y_p, y_s,
            k_p.reshape(1, bp, tp, n_heads, 2, hd), v_p.reshape(1, bp, tp, n_heads, 2 * hd),
            k_s.reshape(1, bs_, ts, n_heads, 2, hd), v_s.reshape(1, bs_, ts, n_heads, 2 * hd),
            gv_s.reshape(1, bs_, ts, d))
```

```python
import functools
import math

import numpy as np
import jax
import jax.numpy as jnp
from jax import lax
from jax.experimental import pallas as pl
from jax.experimental.pallas import tpu as pltpu

F32 = jnp.float32
BF16 = jnp.bfloat16

EPS = 1e-6
CHUNK = 64
GMLP_CHUNK = 128
GMLP_GROUPS = 8
N_MODS = 9
NEG_BIG = -1e30

V7X_VMEM_BYTES = 64 * 1024 * 1024
VMEM_LIMIT_BYTES = V7X_VMEM_BYTES - 8 * 1024 * 1024
LANES = 128
MXU_DIM = 256
ATTN_TQ = 512
ALIBI_SPLIT = 256
ALIBI_PERIOD = 1024


def _lambda_init(layer_idx):
    return 0.8 - 0.6 * math.exp(-0.3 * layer_idx)


def _const_spec(shape):
    nd = len(shape)
    return pl.BlockSpec(shape, lambda *_: (0,) * nd, pipeline_mode=pl.Buffered(1))


def _params(n_axes, flags=None):
    return pltpu.CompilerParams(
        dimension_semantics=("arbitrary",) * n_axes, vmem_limit_bytes=VMEM_LIMIT_BYTES, flags=flags)


def _mods_kernel(c_ref, w_ref, b_ref, o_ref):
    c = c_ref[...]
    h = c * jax.nn.sigmoid(c)
    o_ref[...] = jnp.dot(h, w_ref[...], preferred_element_type=F32,
                         precision=lax.Precision.HIGHEST) + b_ref[...]


def _mods(c_all, ada_w, ada_b):
    n, d = c_all.shape
    width = ada_w.shape[1]
    tn = d
    return pl.pallas_call(
        _mods_kernel,
        grid=(width // tn,),
        in_specs=[pl.BlockSpec((n, d), lambda j: (0, 0)),
                  pl.BlockSpec((d, tn), lambda j: (0, j)),
                  pl.BlockSpec((1, tn), lambda j: (0, j))],
        out_specs=pl.BlockSpec((n, tn), lambda j: (0, j)),
        out_shape=jax.ShapeDtypeStruct((n, width), F32),
        compiler_params=_params(1),
        name="adaln_mods",
    )(c_all, ada_w, ada_b.reshape(1, width))


def _rms(x):
    return x * lax.rsqrt(jnp.mean(x * x, axis=-1, keepdims=True) + EPS)


def _mod_norm(x, g, mods_ref, first):
    sh = mods_ref[0, first:first + 1, :]
    sc = mods_ref[0, first + 1:first + 2, :]
    return (_rms(x) * g) * (1.0 + sc) + sh


def _swiglu(h, wg_ref, wu_ref, wd_ref):
    acc = None
    for c in range(wg_ref.shape[0]):
        gate = jnp.dot(h, wg_ref[c], preferred_element_type=F32)
        up = jnp.dot(h, wu_ref[c], preferred_element_type=F32)
        a = (gate * jax.nn.sigmoid(gate) * up).astype(BF16)
        part = jnp.dot(a, wd_ref[c], preferred_element_type=F32)
        acc = part if acc is None else acc + part
    return acc


def _gelu_tanh(x):
    return 0.5 * x * (1.0 + jnp.tanh(math.sqrt(2.0 / math.pi) * (x + 0.044715 * (x * x * x))))


def _chunk_ffn_weights(w_gu, w_d, cs):
    d, f2 = w_gu.shape
    f = f2 // 2
    n = f // cs
    wg = w_gu[:, :f].reshape(d, n, cs).transpose(1, 0, 2).astype(BF16)
    wu = w_gu[:, f:].reshape(d, n, cs).transpose(1, 0, 2).astype(BF16)
    wd = w_d.reshape(n, cs, d).astype(BF16)
    return wg, wu, wd


def _ffn_kernel(x_ref, mods_ref, ng_ref, wg_ref, wu_ref, wd_ref, o_ref):
    x = x_ref[0]
    h = _mod_norm(x, ng_ref[0:1, :], mods_ref, 0).astype(BF16)
    gt = mods_ref[0, 2:3, :]
    o_ref[0] = x + 0.5 * gt * _swiglu(h, wg_ref, wu_ref, wd_ref)


def _ffn1(x, mods, mods_off, norm_g, wg, wu, wd, tm):
    b, t, d = x.shape
    return pl.pallas_call(
        _ffn_kernel,
        grid=(b, t // tm),
        in_specs=[pl.BlockSpec((1, tm, d), lambda bi, i: (bi, i, 0)),
                  pl.BlockSpec((1, N_MODS, d), lambda bi, i: (bi + mods_off, 0, 0)),
                  _const_spec(norm_g.shape), _const_spec(wg.shape), _const_spec(wu.shape),
                  _const_spec(wd.shape)],
        out_specs=pl.BlockSpec((1, tm, d), lambda bi, i: (bi, i, 0)),
        out_shape=jax.ShapeDtypeStruct((b, t, d), F32),
        compiler_params=_params(2),
        name="ffn1",
    )(x, mods, norm_g, wg, wu, wd)


def _proj_kernel(x_ref, mods_ref, ng_ref, win_ref, wgate_ref, bgate_ref, qg_ref, kg_ref, gmat_ref,
                 vng_ref, ws_ref, bs_ref, wbb_ref,
                 q_o, k_o, v_o, kb_o, vb_o, ga_o, gb_o, *maybe_gv_o, rows, q_scale):
    x = x_ref[0]
    tm, d = x.shape
    h = _mod_norm(x, ng_ref[1:2, :], mods_ref, 3).astype(BF16)

    def group_norm(z, g):
        ms = jnp.dot((z * z).astype(BF16), gmat_ref[...], preferred_element_type=F32)
        return z * lax.rsqrt(ms + EPS) * g

    q = group_norm(jnp.dot(h, win_ref[0], preferred_element_type=F32), qg_ref[...])
    q_o[0] = (q * q_scale).astype(BF16)
    k = group_norm(jnp.dot(h, win_ref[1], preferred_element_type=F32), kg_ref[...])
    k_o[0] = k
    kb_o[0] = k.astype(BF16)
    v = jnp.dot(h, win_ref[2], preferred_element_type=F32)
    v_o[0] = v
    vb_o[0] = v.astype(BF16)

    u = _gelu_tanh(jnp.dot(h, win_ref[3], preferred_element_type=F32))
    gz = _gelu_tanh(jnp.dot(h, win_ref[4], preferred_element_type=F32))
    gv = _rms(gz) * vng_ref[...]
    if maybe_gv_o:
        maybe_gv_o[0][0] = gv
    gvb = gv.astype(BF16)

    row_i = lax.broadcasted_iota(jnp.int32, (rows, rows), 0)
    col_i = lax.broadcasted_iota(jnp.int32, (rows, rows), 1)
    tril = col_i <= row_i
    gw = d // GMLP_GROUPS
    ws = [jnp.where(tril, ws_ref[g], 0.0).astype(BF16) for g in range(GMLP_GROUPS)]
    s_chunks = []
    for c in range(tm // rows):
        r0 = c * rows
        mix = jnp.concatenate(
            [jnp.dot(ws[g], gvb[r0:r0 + rows, g * gw:(g + 1) * gw], preferred_element_type=F32)
             for g in range(GMLP_GROUPS)], axis=1)
        s_chunks.append(u[r0:r0 + rows, :] * (mix + bs_ref[...]))
    s = s_chunks[0] if len(s_chunks) == 1 else jnp.concatenate(s_chunks, axis=0)
    sb = jnp.dot(s.astype(BF16), wbb_ref[...], preferred_element_type=F32)

    gate_a = jax.nn.sigmoid(jnp.dot(h, wgate_ref[0], preferred_element_type=F32) + bgate_ref[0:1, :])
    gate_b = jax.nn.sigmoid(jnp.dot(h, wgate_ref[1], preferred_element_type=F32) + bgate_ref[1:2, :])
    ga_o[0] = gate_a.astype(BF16)
    gb_o[0] = (gate_b * sb).astype(BF16)


def _proj(x, mods, mods_off, norm_g, win, wgate, bgate, qg, kg, gmat, vng, ws, bs, wbb, tm, rows,
          q_scale, want_gv):
    b, t, d = x.shape
    row_spec = pl.BlockSpec((1, tm, d), lambda bi, i: (bi, i, 0))
    out_shapes = [jax.ShapeDtypeStruct((b, t, d), BF16),
                  jax.ShapeDtypeStruct((b, t, d), F32),
                  jax.ShapeDtypeStruct((b, t, d), F32),
                  jax.ShapeDtypeStruct((b, t, d), BF16),
                  jax.ShapeDtypeStruct((b, t, d), BF16),
                  jax.ShapeDtypeStruct((b, t, d), BF16),
                  jax.ShapeDtypeStruct((b, t, d), BF16)]
    if want_gv:
        out_shapes.append(jax.ShapeDtypeStruct((b, t, d), F32))
    consts = (norm_g, win, wgate, bgate, qg, kg, gmat, vng, ws, bs, wbb)
    return pl.pallas_call(
        functools.partial(_proj_kernel, rows=rows, q_scale=q_scale),
        grid=(b, t // tm),
        in_specs=[row_spec, pl.BlockSpec((1, N_MODS, d), lambda bi, i: (bi + mods_off, 0, 0))]
        + [_const_spec(a.shape) for a in consts],
        out_specs=[row_spec] * len(out_shapes),
        out_shape=out_shapes,
        compiler_params=_params(2),
        name="mixer_proj",
    )(x, mods, *consts)


def _split_maps(q):
    lane = lax.broadcasted_iota(jnp.int32, q.shape, 1)
    half = q.shape[1] // 2
    zero = jnp.zeros_like(q)
    return jnp.where(lane < half, q, zero), jnp.where(lane >= half, q, zero)


def _qk(a, k):
    return lax.dot_general(a, k, (((1,), (1,)), ((), ())), preferred_element_type=F32)


def _lambda(lq_ref, lam_init):
    lq = lq_ref[...]
    t1 = jnp.sum(lq[0:1, :] * lq[1:2, :], axis=-1, keepdims=True)
    t2 = jnp.sum(lq[2:3, :] * lq[3:4, :], axis=-1, keepdims=True)
    return jnp.exp(t1) - jnp.exp(t2) + lam_init


def _finish_heads(acc1, l1, acc2, l2, lam, sg, lam_init):
    o = acc1 / l1 - lam * (acc2 / l2)
    return (_rms(o) * sg) * (1.0 - lam_init)


def _attn_kernel(slope_ref, tile_ref, period_ref, bias_ref, first_ref, last_ref, dist_ref,
                 q_ref, k_ref, v_ref, eqt_ref, ek_ref, lq_ref, sg_ref, o_ref,
                 kx_scr, vt_scr, bias_scr, at_scr, s_scr, p_last_scr, smax_scr, m_scr, fin_scr,
                 acc_scr, *, lam_init, wide, tq, n_steps):
    h = pl.program_id(1)
    t_all = k_ref.shape[1]
    slope = slope_ref[h]
    per_wide = wide // tq

    kx_scr[:, 0:LANES] = k_ref[0]
    for c in range(t_all // wide):
        kx_scr[c * wide:(c + 1) * wide, LANES:2 * LANES] = ek_ref[0]
        vt_scr[0:LANES, c * wide:(c + 1) * wide] = (
            v_ref[0, c * wide:(c + 1) * wide, :].astype(F32).T.astype(BF16))
    vt_scr[LANES:, :] = jnp.ones((vt_scr.shape[0] - LANES, t_all), BF16)
    s_pos = lax.broadcasted_iota(jnp.int32, (wide, tq), 0)
    for c in range(per_wide):
        t_pos = c * tq + lax.broadcasted_iota(jnp.int32, (wide, tq), 1)
        visible = (s_pos // CHUNK) <= (t_pos // CHUNK)
        fix = (jnp.abs(t_pos - s_pos) - (t_pos - s_pos)).astype(F32)
        bias_scr[c] = jnp.where(visible, -slope * fix, NEG_BIG)
    p_last_scr[...] = jnp.zeros(p_last_scr.shape, BF16)
    acc_scr[...] = jnp.zeros(acc_scr.shape, F32)
    m_scr[...] = jnp.full(m_scr.shape, NEG_BIG, F32)

    def build_queries(tile):
        r0 = pl.multiple_of(tile * tq, tq)
        qt = q_ref[0, pl.ds(r0, tq), :].astype(F32).T
        row = lax.broadcasted_iota(jnp.int32, qt.shape, 0)
        zero = jnp.zeros_like(qt)
        at_scr[0, 0:LANES, :] = jnp.where(row < LANES // 2, qt, zero).astype(BF16)
        at_scr[1, 0:LANES, :] = jnp.where(row >= LANES // 2, qt, zero).astype(BF16)
        alibi_rows = eqt_ref[0, tile % per_wide]
        at_scr[0, LANES:2 * LANES, :] = alibi_rows
        at_scr[1, LANES:2 * LANES, :] = alibi_rows

    sub = MXU_DIM
    n_sub = wide // sub

    def scores(e, slot, masked):
        kx = kx_scr[pl.ds(pl.multiple_of(period_ref[e] * wide, wide), wide), :]
        for mp in range(2):
            s = jnp.dot(kx, at_scr[mp], preferred_element_type=F32)
            if masked:
                s = s + bias_scr[bias_ref[e]]
            s_scr[slot, mp] = s
            smax_scr[slot, mp, 0:1, :] = jnp.max(s, axis=0, keepdims=True)

    def values_block(e, c):
        off = pl.multiple_of(period_ref[e] * wide + c * sub, sub)
        return vt_scr[:, pl.ds(off, sub)]

    def pipeline_step(e, slot, masked):
        shift = -slope * (dist_ref[e] * wide).astype(F32)
        is_first = first_ref[e] == 1
        offsets, alphas = [], []
        for mp in range(2):
            m_old = jnp.where(is_first, NEG_BIG, m_scr[mp, 0:1, :])
            m_new = jnp.maximum(m_old, smax_scr[slot, mp, 0:1, :] + shift)
            alphas.append(jnp.exp2(m_old - m_new))
            m_scr[mp, 0:1, :] = m_new
            offsets.append(m_new - shift)
        vt_pending = values_block(e - 1, n_sub - 1)
        pending = [jnp.dot(vt_pending, p_last_scr[mp], preferred_element_type=F32) for mp in range(2)]
        scores(e + 1, 1 - slot, masked)
        pv = [None, None]
        for c in range(n_sub):
            rows = slice(c * sub, (c + 1) * sub)
            for mp in range(2):
                p = jnp.exp2(s_scr[slot, mp, rows, :] - offsets[mp]).astype(BF16)
                if c == n_sub - 1:
                    p_last_scr[mp] = p
                else:
                    part = jnp.dot(values_block(e, c), p, preferred_element_type=F32)
                    pv[mp] = part if pv[mp] is None else pv[mp] + part
        for mp in range(2):
            done = acc_scr[mp] + pending[mp]
            fin_scr[mp] = done
            acc_scr[mp] = alphas[mp] * done + pv[mp]

    def finalize(tile):
        lam = _lambda(lq_ref, lam_init)
        a1 = fin_scr[0]
        a2 = fin_scr[1]
        ot = a1[0:LANES] / a1[LANES:LANES + 1] - lam * (a2[0:LANES] / a2[LANES:LANES + 1])
        ot = ot * lax.rsqrt(jnp.mean(ot * ot, axis=0, keepdims=True) + EPS)
        r0 = pl.multiple_of(tile * tq, tq)
        o_ref[0, pl.ds(r0, tq), :] = ((ot.T * sg_ref[...]) * (1.0 - lam_init)).astype(o_ref.dtype)

    build_queries(tile_ref[1])
    scores(1, 0, True)

    def iteration(k, carry):
        e = k + 1

        @pl.when(first_ref[e + 1] == 1)
        def _():
            build_queries(tile_ref[e + 1])

        next_masked = bias_ref[e + 1] < per_wide
        for slot in (0, 1):
            for masked in (False, True):
                cond = next_masked if masked else jnp.logical_not(next_masked)

                @pl.when(jnp.logical_and(k % 2 == slot, cond))
                def _():
                    pipeline_step(e, slot, masked)

        @pl.when(last_ref[e - 1] == 1)
        def _():
            finalize(tile_ref[e - 1])

        return carry

    lax.fori_loop(0, n_steps + 1, iteration, 0)


def _bf16_pieces(x, n):
    pieces, rest = [], np.asarray(x, np.float32)
    for _ in range(n):
        p = rest.astype(BF16).astype(np.float32)
        pieces.append(p)
        rest = rest - p
    return pieces


def _alibi_rows_cols(slopes, period):
    n_heads = slopes.shape[0]
    pos = np.arange(period)
    offs = ((pos // ALIBI_SPLIT) * ALIBI_SPLIT, pos % ALIBI_SPLIT)
    eqt = np.zeros((n_heads, LANES, period), np.float32)
    ek = np.zeros((n_heads, period, LANES), np.float32)
    lane = 0
    for piece in _bf16_pieces(slopes, 3):
        for off in offs:
            eqt[:, lane, :] = off[None, :]
            ek[:, :, lane] = -piece[:, None]
            eqt[:, lane + 1, :] = piece[:, None]
            ek[:, :, lane + 1] = off[None, :]
            lane += 2
    return jnp.asarray(eqt, BF16), jnp.asarray(ek, BF16)


def _attention_schedule(n_tiles, per_wide):
    tile, period, bias, first, last, dist = [], [], [], [], [], []
    for i in range(n_tiles):
        n_wide = i // per_wide
        for j in range(n_wide + 1):
            own = j == n_wide
            tile.append(i)
            period.append(j)
            bias.append(i % per_wide if own else per_wide)
            first.append(int(j == 0))
            last.append(int(own))
            dist.append(n_wide - j)
    pad = lambda v, fill: np.asarray([fill] + v + [fill, fill], np.int32)
    return (pad(tile, n_tiles - 1), pad(period, 0), pad(bias, per_wide), pad(first, 1),
            pad(last, 0), pad(dist, 0)), len(tile)


def _attention(q, kb, vb, slopes, lq, sg, lam_init):
    b, t, width = q.shape
    n_heads = width // LANES
    tq = min(ATTN_TQ, t)
    wide = min(ALIBI_PERIOD, t)
    assert t % wide == 0 and wide % tq == 0
    per_wide = wide // tq
    eqt, ek = _alibi_rows_cols(slopes, wide)
    eqt = eqt.reshape(n_heads, LANES, per_wide, tq).transpose(0, 2, 1, 3)
    tables, n_steps = _attention_schedule(t // tq, per_wide)
    ones_rows = 16
    n_prefetch = 1 + len(tables)
    head_spec = pl.BlockSpec((1, t, LANES), lambda bi, h, *_: (bi, 0, h))
    grid_spec = pltpu.PrefetchScalarGridSpec(
        num_scalar_prefetch=n_prefetch,
        grid=(b, n_heads),
        in_specs=[head_spec, head_spec, head_spec,
                  pl.BlockSpec((1, per_wide, LANES, tq), lambda bi, h, *_: (h, 0, 0, 0)),
                  pl.BlockSpec((1, wide, LANES), lambda bi, h, *_: (h, 0, 0)),
                  pl.BlockSpec(lq.shape, lambda bi, h, *_: (0, 0)),
                  pl.BlockSpec(sg.shape, lambda bi, h, *_: (0, 0))],
        out_specs=head_spec,
        scratch_shapes=[pltpu.VMEM((t, 2 * LANES), BF16),
                        pltpu.VMEM((LANES + ones_rows, t), BF16),
                        pltpu.VMEM((per_wide, wide, tq), F32),
                        pltpu.VMEM((2, 2 * LANES, tq), BF16),
                        pltpu.VMEM((2, 2, wide, tq), F32),
                        pltpu.VMEM((2, MXU_DIM, tq), BF16),
                        pltpu.VMEM((2, 2, 8, tq), F32),
                        pltpu.VMEM((2, 8, tq), F32),
                        pltpu.VMEM((2, LANES + ones_rows, tq), F32),
                        pltpu.VMEM((2, LANES + ones_rows, tq), F32)],
    )
    return pl.pallas_call(
        functools.partial(_attn_kernel, lam_init=lam_init, wide=wide, tq=tq, n_steps=n_steps),
        grid_spec=grid_spec,
        out_shape=jax.ShapeDtypeStruct((b, t, width), BF16),
        compiler_params=_params(2),
        name="diff_attention",
    )(jnp.asarray(slopes, F32), *tables, q, kb, vb, eqt, ek, lq, sg)


def _attn_cache_kernel(slope_ref, q_ref, kn_ref, vn_ref, ck_ref, cv_ref, lq_ref, sg_ref, o_ref, *,
                       lam_init):
    h = pl.program_id(1)
    slope = slope_ref[h]
    tn = q_ref.shape[1]
    past = ck_ref.shape[1]
    q1, q2 = _split_maps(q_ref[0])
    ck = ck_ref[0].astype(BF16)
    cv = cv_ref[0].astype(BF16)
    kn = kn_ref[0]
    vn = vn_ref[0]

    def bias_for(k0, width):
        qpos = past + lax.broadcasted_iota(jnp.int32, (tn, width), 0)
        kpos = k0 + lax.broadcasted_iota(jnp.int32, (tn, width), 1)
        visible = (kpos // CHUNK) <= (qpos // CHUNK)
        return jnp.where(visible, -slope * jnp.abs(qpos - kpos).astype(F32), NEG_BIG)

    bias_c = bias_for(0, past)
    bias_n = bias_for(past, tn)

    def one_map(a):
        sc = _qk(a, ck) + bias_c
        sn = _qk(a, kn) + bias_n
        m = jnp.maximum(jnp.max(sc, axis=-1, keepdims=True), jnp.max(sn, axis=-1, keepdims=True))
        pc = jnp.exp2(sc - m)
        pn = jnp.exp2(sn - m)
        l = jnp.sum(pc, axis=-1, keepdims=True) + jnp.sum(pn, axis=-1, keepdims=True)
        acc = (jnp.dot(pc.astype(BF16), cv, preferred_element_type=F32)
               + jnp.dot(pn.astype(BF16), vn, preferred_element_type=F32))
        return l, acc

    l1, acc1 = one_map(q1)
    l2, acc2 = one_map(q2)
    lam = _lambda(lq_ref, lam_init)
    o_ref[0] = _finish_heads(acc1, l1, acc2, l2, lam, sg_ref[...], lam_init).astype(o_ref.dtype)


def _attention_cached(q, kb, vb, cache_k, cache_v, slopes, lq, sg, lam_init):
    b, tn, width = q.shape
    past = cache_k.shape[1]
    n_heads = width // LANES
    new_spec = pl.BlockSpec((1, tn, LANES), lambda bi, h, s: (bi, 0, h))
    past_spec = pl.BlockSpec((1, past, LANES), lambda bi, h, s: (bi, 0, h))
    grid_spec = pltpu.PrefetchScalarGridSpec(
        num_scalar_prefetch=1,
        grid=(b, n_heads),
        in_specs=[new_spec, new_spec, new_spec, past_spec, past_spec,
                  pl.BlockSpec(lq.shape, lambda bi, h, s: (0, 0)),
                  pl.BlockSpec(sg.shape, lambda bi, h, s: (0, 0))],
        out_specs=new_spec,
    )
    return pl.pallas_call(
        functools.partial(_attn_cache_kernel, lam_init=lam_init),
        grid_spec=grid_spec,
        out_shape=jax.ShapeDtypeStruct((b, tn, width), BF16),
        compiler_params=_params(2),
        name="diff_attention_cached",
    )(jnp.asarray(slopes, F32), q, kb, vb, cache_k, cache_v, lq, sg)


def _out_kernel(x_ref, o_ref, ga_ref, gb_ref, mods_ref, ng_ref, wba_ref, wout_ref, wg_ref, wu_ref,
                wd_ref, y_ref):
    x = x_ref[0]
    oa = jnp.dot(o_ref[0], wba_ref[...], preferred_element_type=F32)
    mixed = (ga_ref[0].astype(F32) * oa + gb_ref[0].astype(F32)).astype(BF16)
    x = x + mods_ref[0, 5:6, :] * jnp.dot(mixed, wout_ref[...], preferred_element_type=F32)
    h = _mod_norm(x, ng_ref[2:3, :], mods_ref, 6).astype(BF16)
    x = x + 0.5 * mods_ref[0, 8:9, :] * _swiglu(h, wg_ref, wu_ref, wd_ref)
    y_ref[0] = _rms(x) * ng_ref[3:4, :]


def _mix_ffn2(x, o, ga, gb, mods, mods_off, norm_g, wba, wout, wg, wu, wd, tm):
    b, t, d = x.shape
    row_spec = pl.BlockSpec((1, tm, d), lambda bi, i: (bi, i, 0))
    consts = (norm_g, wba, wout, wg, wu, wd)
    return pl.pallas_call(
        _out_kernel,
        grid=(b, t // tm),
        in_specs=[row_spec] * 4 + [pl.BlockSpec((1, N_MODS, d), lambda bi, i: (bi + mods_off, 0, 0))]
        + [_const_spec(a.shape) for a in consts],
        out_specs=row_spec,
        out_shape=jax.ShapeDtypeStruct((b, t, d), F32),
        compiler_params=_params(2),
        name="mix_ffn2",
    )(x, o, ga, gb, mods, *consts)


def _row_tile(t, target):
    tm = min(t, target)
    while t % tm:
        tm //= 2
    return tm


def kernel(x_prompt, x_sample, cache_k, cache_v, c_prompt, c_sample, ada_w, ada_b, norm_g, ffn1_wgu,
           ffn1_wd, w_in, q_norm_g, k_norm_g, lambda_qk, attn_subln_g, gmlp_vnorm_g, gmlp_ws, gmlp_bs,
           w_gate, b_gate, w_branch, w_out, ffn2_wgu, ffn2_wd):
    assert ada_w.shape[0] == 1, "single-layer step"
    bp, tp, d = x_prompt.shape
    bs_, ts, _ = x_sample.shape
    _, _, past, n_heads, _, hd = cache_k.shape
    assert n_heads == 8 and 2 * hd == LANES and n_heads * LANES == d
    lam_init = _lambda_init(0)
    log2e = math.log2(math.e)
    slopes = (2.0 ** (-8.0 * np.arange(1, n_heads + 1) / n_heads) * log2e).astype(np.float32)
    q_scale = float(hd) ** -0.5 * log2e

    ffn_cs = 256
    wg1, wu1, wd1 = _chunk_ffn_weights(ffn1_wgu[0], ffn1_wd[0], ffn_cs)
    wg2, wu2, wd2 = _chunk_ffn_weights(ffn2_wgu[0], ffn2_wd[0], ffn_cs)
    win = w_in[0].reshape(d, 5, d).transpose(1, 0, 2).astype(BF16)
    wgate = w_gate[0].reshape(d, 2, d).transpose(1, 0, 2).astype(BF16)
    bgate = b_gate[0].reshape(2, d)
    wba = w_branch[0, :d].astype(BF16)
    wbb = w_branch[0, d:].astype(BF16)
    wout = w_out[0].astype(BF16)
    ng = norm_g[0]
    qg = jnp.tile(q_norm_g[0], d // hd).reshape(1, d)
    kg = jnp.tile(k_norm_g[0], d // hd).reshape(1, d)
    grp = np.arange(d) // hd
    gmat = jnp.asarray((grp[:, None] == grp[None, :]).astype(np.float32) / hd, BF16)
    vng = gmlp_vnorm_g[0].reshape(1, d)
    sg = attn_subln_g[0].reshape(1, LANES)
    lq = lambda_qk[0]

    mods = _mods(jnp.concatenate([c_prompt, c_sample], axis=0), ada_w[0], ada_b[0])
    mods = mods.reshape(bp + bs_, N_MODS, d)

    def gmlp_consts(t):
        rows = min(t, GMLP_CHUNK)
        ws = gmlp_ws[0][:, :rows, :rows]
        bs = jnp.repeat(gmlp_bs[0][:, :rows].T, d // GMLP_GROUPS, axis=1)
        return rows, ws, bs

    def layer(x, mods_off, attend, tm_ffn, tm_proj, want_gv):
        b, t, _ = x.shape
        rows, ws, bs = gmlp_consts(t)
        x1 = _ffn1(x, mods, mods_off, ng, wg1, wu1, wd1, _row_tile(t, tm_ffn))
        tmp = max(_row_tile(t, tm_proj), rows)
        outs = _proj(x1, mods, mods_off, ng, win, wgate, bgate, qg, kg, gmat, vng, ws, bs, wbb,
                     tmp, rows, q_scale, want_gv)
        q, k, v, kb, vb, ga, gb = outs[:7]
        o = attend(q, kb, vb)
        y = _mix_ffn2(x1, o, ga, gb, mods, mods_off, ng, wba, wout, wg2, wu2, wd2,
                      _row_tile(t, tm_ffn))
        return y, k, v, (outs[7] if want_gv else None)

    y_p, k_p, v_p, _ = layer(
        x_prompt, 0, lambda q, kb, vb: _attention(q, kb, vb, slopes, lq, sg, lam_init),
        512, 256, False)

    ck = cache_k[0].reshape(bs_, past, d)
    cv = cache_v[0].reshape(bs_, past, d)
    y_s, k_s, v_s, gv_s = layer(
        x_sample, bp,
        lambda q, kb, vb: _attention_cached(q, kb, vb, ck, cv, slopes, lq, sg, lam_init),
        512, 256, True)

    return (y_p, y_s,
            k_p.reshape(1, bp, tp, n_heads, 2, hd), v_p.reshape(1, bp, tp, n_heads, 2 * hd),
            k_s.reshape(1, bs_, ts, n_heads, 2, hd), v_s.reshape(1, bs_, ts, n_heads, 2 * hd),
            gv_s.reshape(1, bs_, ts, d))
```

```python
import functools
import math

import numpy as np
import jax
import jax.numpy as jnp
from jax import lax
from jax.experimental import pallas as pl
from jax.experimental.pallas import tpu as pltpu

F32 = jnp.float32
BF16 = jnp.bfloat16

EPS = 1e-6
CHUNK = 64
GMLP_CHUNK = 128
GMLP_GROUPS = 8
N_MODS = 9
NEG_BIG = -1e30

V7X_VMEM_BYTES = 64 * 1024 * 1024
VMEM_LIMIT_BYTES = V7X_VMEM_BYTES - 8 * 1024 * 1024
LANES = 128
MXU_DIM = 256
ATTN_TQ = 512
FFN_CHUNK = 256
ALIBI_SPLIT = 256
ALIBI_PERIOD = 1024


def _lambda_init(layer_idx):
    return 0.8 - 0.6 * math.exp(-0.3 * layer_idx)


def _const_spec(shape):
    nd = len(shape)
    return pl.BlockSpec(shape, lambda *_: (0,) * nd, pipeline_mode=pl.Buffered(1))


def _params(n_axes, flags=None):
    return pltpu.CompilerParams(
        dimension_semantics=("arbitrary",) * n_axes, vmem_limit_bytes=VMEM_LIMIT_BYTES, flags=flags)


def _mods_kernel(c_ref, w_ref, b_ref, o_ref):
    c = c_ref[...]
    h = c * jax.nn.sigmoid(c)
    o_ref[...] = jnp.dot(h, w_ref[...], preferred_element_type=F32,
                         precision=lax.Precision.HIGHEST) + b_ref[...]


def _mods(c_all, ada_w, ada_b):
    n, d = c_all.shape
    width = ada_w.shape[1]
    tn = d
    return pl.pallas_call(
        _mods_kernel,
        grid=(width // tn,),
        in_specs=[pl.BlockSpec((n, d), lambda j: (0, 0)),
                  pl.BlockSpec((d, tn), lambda j: (0, j)),
                  pl.BlockSpec((1, tn), lambda j: (0, j))],
        out_specs=pl.BlockSpec((n, tn), lambda j: (0, j)),
        out_shape=jax.ShapeDtypeStruct((n, width), F32),
        compiler_params=_params(1),
        name="adaln_mods",
    )(c_all, ada_w, ada_b.reshape(1, width))


def _rms(x):
    return x * lax.rsqrt(jnp.mean(x * x, axis=-1, keepdims=True) + EPS)


def _mod_norm(x, g, mods_ref, first):
    sh = mods_ref[0, first:first + 1, :]
    sc = mods_ref[0, first + 1:first + 2, :]
    return (_rms(x) * g) * (1.0 + sc) + sh


def _swiglu(h, wgu_ref, wd_ref):
    f = wd_ref.shape[0]
    acc = None
    for c0 in range(0, f, FFN_CHUNK):
        gate = jnp.dot(h, wgu_ref[:, c0:c0 + FFN_CHUNK], preferred_element_type=F32)
        up = jnp.dot(h, wgu_ref[:, f + c0:f + c0 + FFN_CHUNK], preferred_element_type=F32)
        a = (gate * jax.nn.sigmoid(gate) * up).astype(BF16)
        part = jnp.dot(a, wd_ref[c0:c0 + FFN_CHUNK, :], preferred_element_type=F32)
        acc = part if acc is None else acc + part
    return acc


def _gelu_tanh(x):
    return 0.5 * x * (1.0 + jnp.tanh(math.sqrt(2.0 / math.pi) * (x + 0.044715 * (x * x * x))))


def _ffn_kernel(x_ref, mods_ref, ng_ref, wgu_ref, wd_ref, o_ref):
    x = x_ref[0]
    h = _mod_norm(x, ng_ref[0:1, :], mods_ref, 0).astype(BF16)
    gt = mods_ref[0, 2:3, :]
    o_ref[0] = x + 0.5 * gt * _swiglu(h, wgu_ref, wd_ref)


def _ffn1(x, mods, mods_off, norm_g, wgu, wd, tm):
    b, t, d = x.shape
    return pl.pallas_call(
        _ffn_kernel,
        grid=(b, t // tm),
        in_specs=[pl.BlockSpec((1, tm, d), lambda bi, i: (bi, i, 0)),
                  pl.BlockSpec((1, N_MODS, d), lambda bi, i: (bi + mods_off, 0, 0)),
                  _const_spec(norm_g.shape), _const_spec(wgu.shape), _const_spec(wd.shape)],
        out_specs=pl.BlockSpec((1, tm, d), lambda bi, i: (bi, i, 0)),
        out_shape=jax.ShapeDtypeStruct((b, t, d), F32),
        compiler_params=_params(2),
        name="ffn1",
    )(x, mods, norm_g, wgu, wd)


def _proj_kernel(x_ref, mods_ref, ng_ref, win_ref, wgate_ref, bgate_ref, qg_ref, kg_ref, gmat_ref,
                 vng_ref, ws_ref, bs_ref, wbb_ref,
                 q_o, k_o, v_o, kb_o, vb_o, ga_o, gb_o, *maybe_gv_o, rows, q_scale):
    x = x_ref[0]
    tm, d = x.shape
    h = _mod_norm(x, ng_ref[1:2, :], mods_ref, 3).astype(BF16)

    def group_norm(z, g):
        ms = jnp.dot((z * z).astype(BF16), gmat_ref[...], preferred_element_type=F32)
        return z * lax.rsqrt(ms + EPS) * g

    def w_in(j):
        return win_ref[:, j * d:(j + 1) * d]

    q = group_norm(jnp.dot(h, w_in(0), preferred_element_type=F32), qg_ref[...])
    q_o[0] = (q * q_scale).astype(BF16)
    k = group_norm(jnp.dot(h, w_in(1), preferred_element_type=F32), kg_ref[...])
    k_o[0] = k
    kb_o[0] = k.astype(BF16)
    v = jnp.dot(h, w_in(2), preferred_element_type=F32)
    v_o[0] = v
    vb_o[0] = v.astype(BF16)

    u = _gelu_tanh(jnp.dot(h, w_in(3), preferred_element_type=F32))
    gz = _gelu_tanh(jnp.dot(h, w_in(4), preferred_element_type=F32))
    gv = _rms(gz) * vng_ref[...]
    if maybe_gv_o:
        maybe_gv_o[0][0] = gv
    gvb = gv.astype(BF16)

    row_i = lax.broadcasted_iota(jnp.int32, (rows, rows), 0)
    col_i = lax.broadcasted_iota(jnp.int32, (rows, rows), 1)
    tril = col_i <= row_i
    gw = d // GMLP_GROUPS
    ws = [jnp.where(tril, ws_ref[g], 0.0).astype(BF16) for g in range(GMLP_GROUPS)]
    s_chunks = []
    for c in range(tm // rows):
        r0 = c * rows
        mix = jnp.concatenate(
            [jnp.dot(ws[g], gvb[r0:r0 + rows, g * gw:(g + 1) * gw], preferred_element_type=F32)
             for g in range(GMLP_GROUPS)], axis=1)
        s_chunks.append(u[r0:r0 + rows, :] * (mix + bs_ref[...]))
    s = s_chunks[0] if len(s_chunks) == 1 else jnp.concatenate(s_chunks, axis=0)
    sb = jnp.dot(s.astype(BF16), wbb_ref[...], preferred_element_type=F32)

    gate_a = jax.nn.sigmoid(
        jnp.dot(h, wgate_ref[:, 0:d], preferred_element_type=F32) + bgate_ref[0:1, :])
    gate_b = jax.nn.sigmoid(
        jnp.dot(h, wgate_ref[:, d:2 * d], preferred_element_type=F32) + bgate_ref[1:2, :])
    ga_o[0] = gate_a.astype(BF16)
    gb_o[0] = (gate_b * sb).astype(BF16)


def _proj(x, mods, mods_off, norm_g, win, wgate, bgate, qg, kg, gmat, vng, ws, bs, wbb, tm, rows,
          q_scale, want_gv):
    b, t, d = x.shape
    row_spec = pl.BlockSpec((1, tm, d), lambda bi, i: (bi, i, 0))
    out_shapes = [jax.ShapeDtypeStruct((b, t, d), BF16),
                  jax.ShapeDtypeStruct((b, t, d), F32),
                  jax.ShapeDtypeStruct((b, t, d), F32),
                  jax.ShapeDtypeStruct((b, t, d), BF16),
                  jax.ShapeDtypeStruct((b, t, d), BF16),
                  jax.ShapeDtypeStruct((b, t, d), BF16),
                  jax.ShapeDtypeStruct((b, t, d), BF16)]
    if want_gv:
        out_shapes.append(jax.ShapeDtypeStruct((b, t, d), F32))
    consts = (norm_g, win, wgate, bgate, qg, kg, gmat, vng, ws, bs, wbb)
    return pl.pallas_call(
        functools.partial(_proj_kernel, rows=rows, q_scale=q_scale),
        grid=(b, t // tm),
        in_specs=[row_spec, pl.BlockSpec((1, N_MODS, d), lambda bi, i: (bi + mods_off, 0, 0))]
        + [_const_spec(a.shape) for a in consts],
        out_specs=[row_spec] * len(out_shapes),
        out_shape=out_shapes,
        compiler_params=_params(2),
        name="mixer_proj",
    )(x, mods, *consts)


def _split_maps(q):
    lane = lax.broadcasted_iota(jnp.int32, q.shape, 1)
    half = q.shape[1] // 2
    zero = jnp.zeros_like(q)
    return jnp.where(lane < half, q, zero), jnp.where(lane >= half, q, zero)


def _qk(a, k):
    return lax.dot_general(a, k, (((1,), (1,)), ((), ())), preferred_element_type=F32)


def _lambda(lq_ref, lam_init):
    lq = lq_ref[...]
    t1 = jnp.sum(lq[0:1, :] * lq[1:2, :], axis=-1, keepdims=True)
    t2 = jnp.sum(lq[2:3, :] * lq[3:4, :], axis=-1, keepdims=True)
    return jnp.exp(t1) - jnp.exp(t2) + lam_init


def _finish_heads(acc1, l1, acc2, l2, lam, sg, lam_init):
    o = acc1 / l1 - lam * (acc2 / l2)
    return (_rms(o) * sg) * (1.0 - lam_init)


def _attn_kernel(slope_ref, tile_ref, period_ref, bias_ref, first_ref, last_ref, dist_ref,
                 q_ref, k_ref, v_ref, eqt_ref, ek_ref, lq_ref, sg_ref, o_ref,
                 kx_scr, vt_scr, bias_scr, at_scr, s_scr, p_last_scr, smax_scr, m_scr, fin_scr,
                 acc_scr, *, lam_init, wide, tq, n_steps):
    h = pl.program_id(1)
    t_all = k_ref.shape[1]
    slope = slope_ref[h]
    per_wide = wide // tq

    kx_scr[:, 0:LANES] = k_ref[0]
    for c in range(t_all // wide):
        kx_scr[c * wide:(c + 1) * wide, LANES:2 * LANES] = ek_ref[0]
        vt_scr[0:LANES, c * wide:(c + 1) * wide] = (
            v_ref[0, c * wide:(c + 1) * wide, :].astype(F32).T.astype(BF16))
    vt_scr[LANES:, :] = jnp.ones((vt_scr.shape[0] - LANES, t_all), BF16)
    s_pos = lax.broadcasted_iota(jnp.int32, (wide, tq), 0)
    for c in range(per_wide):
        t_pos = c * tq + lax.broadcasted_iota(jnp.int32, (wide, tq), 1)
        visible = (s_pos // CHUNK) <= (t_pos // CHUNK)
        fix = (jnp.abs(t_pos - s_pos) - (t_pos - s_pos)).astype(F32)
        bias_scr[c] = jnp.where(visible, -slope * fix, NEG_BIG)
    p_last_scr[...] = jnp.zeros(p_last_scr.shape, BF16)
    acc_scr[...] = jnp.zeros(acc_scr.shape, F32)
    m_scr[...] = jnp.full(m_scr.shape, NEG_BIG, F32)

    def build_queries(tile):
        r0 = pl.multiple_of(tile * tq, tq)
        qt = q_ref[0, pl.ds(r0, tq), :].astype(F32).T
        row = lax.broadcasted_iota(jnp.int32, qt.shape, 0)
        zero = jnp.zeros_like(qt)
        at_scr[0, 0:LANES, :] = jnp.where(row < LANES // 2, qt, zero).astype(BF16)
        at_scr[1, 0:LANES, :] = jnp.where(row >= LANES // 2, qt, zero).astype(BF16)
        alibi_rows = eqt_ref[0, tile % per_wide]
        at_scr[0, LANES:2 * LANES, :] = alibi_rows
        at_scr[1, LANES:2 * LANES, :] = alibi_rows

    sub = MXU_DIM
    n_sub = wide // sub

    def scores(e, slot, masked):
        kx = kx_scr[pl.ds(pl.multiple_of(period_ref[e] * wide, wide), wide), :]
        for mp in range(2):
            s = jnp.dot(kx, at_scr[mp], preferred_element_type=F32)
            if masked:
                s = s + bias_scr[bias_ref[e]]
            s_scr[slot, mp] = s
            smax_scr[slot, mp, 0:1, :] = jnp.max(s, axis=0, keepdims=True)

    def values_block(e, c):
        off = pl.multiple_of(period_ref[e] * wide + c * sub, sub)
        return vt_scr[:, pl.ds(off, sub)]

    def pipeline_step(e, slot, masked):
        shift = -slope * (dist_ref[e] * wide).astype(F32)
        is_first = first_ref[e] == 1
        offsets, alphas = [], []
        for mp in range(2):
            m_old = jnp.where(is_first, NEG_BIG, m_scr[mp, 0:1, :])
            m_new = jnp.maximum(m_old, smax_scr[slot, mp, 0:1, :] + shift)
            alphas.append(jnp.exp2(m_old - m_new))
            m_scr[mp, 0:1, :] = m_new
            offsets.append(m_new - shift)
        vt_pending = values_block(e - 1, n_sub - 1)
        pending = [jnp.dot(vt_pending, p_last_scr[mp], preferred_element_type=F32) for mp in range(2)]
        scores(e + 1, 1 - slot, masked)
        pv = [None, None]
        for c in range(n_sub):
            rows = slice(c * sub, (c + 1) * sub)
            for mp in range(2):
                p = jnp.exp2(s_scr[slot, mp, rows, :] - offsets[mp]).astype(BF16)
                if c == n_sub - 1:
                    p_last_scr[mp] = p
                else:
                    part = jnp.dot(values_block(e, c), p, preferred_element_type=F32)
                    pv[mp] = part if pv[mp] is None else pv[mp] + part
        for mp in range(2):
            done = acc_scr[mp] + pending[mp]
            fin_scr[mp] = done
            acc_scr[mp] = alphas[mp] * done + pv[mp]

    def finalize(tile):
        lam = _lambda(lq_ref, lam_init)
        a1 = fin_scr[0]
        a2 = fin_scr[1]
        ot = a1[0:LANES] / a1[LANES:LANES + 1] - lam * (a2[0:LANES] / a2[LANES:LANES + 1])
        ot = ot * lax.rsqrt(jnp.mean(ot * ot, axis=0, keepdims=True) + EPS)
        r0 = pl.multiple_of(tile * tq, tq)
        o_ref[0, pl.ds(r0, tq), :] = ((ot.T * sg_ref[...]) * (1.0 - lam_init)).astype(o_ref.dtype)

    build_queries(tile_ref[1])
    scores(1, 0, True)

    def iteration(k, carry):
        e = k + 1

        @pl.when(first_ref[e + 1] == 1)
        def _():
            build_queries(tile_ref[e + 1])

        next_masked = bias_ref[e + 1] < per_wide
        for slot in (0, 1):
            for masked in (False, True):
                cond = next_masked if masked else jnp.logical_not(next_masked)

                @pl.when(jnp.logical_and(k % 2 == slot, cond))
                def _():
                    pipeline_step(e, slot, masked)

        @pl.when(last_ref[e - 1] == 1)
        def _():
            finalize(tile_ref[e - 1])

        return carry

    lax.fori_loop(0, n_steps + 1, iteration, 0)


def _bf16_pieces(x, n):
    pieces, rest = [], np.asarray(x, np.float32)
    for _ in range(n):
        p = rest.astype(BF16).astype(np.float32)
        pieces.append(p)
        rest = rest - p
    return pieces


def _alibi_rows_cols(slopes, period):
    n_heads = slopes.shape[0]
    pos = np.arange(period)
    offs = ((pos // ALIBI_SPLIT) * ALIBI_SPLIT, pos % ALIBI_SPLIT)
    eqt = np.zeros((n_heads, LANES, period), np.float32)
    ek = np.zeros((n_heads, period, LANES), np.float32)
    lane = 0
    for piece in _bf16_pieces(slopes, 3):
        for off in offs:
            eqt[:, lane, :] = off[None, :]
            ek[:, :, lane] = -piece[:, None]
            eqt[:, lane + 1, :] = piece[:, None]
            ek[:, :, lane + 1] = off[None, :]
            lane += 2
    return jnp.asarray(eqt, BF16), jnp.asarray(ek, BF16)


def _attention_schedule(n_tiles, per_wide):
    tile, period, bias, first, last, dist = [], [], [], [], [], []
    for i in range(n_tiles):
        n_wide = i // per_wide
        for j in range(n_wide + 1):
            own = j == n_wide
            tile.append(i)
            period.append(j)
            bias.append(i % per_wide if own else per_wide)
            first.append(int(j == 0))
            last.append(int(own))
            dist.append(n_wide - j)
    pad = lambda v, fill: np.asarray([fill] + v + [fill, fill], np.int32)
    return (pad(tile, n_tiles - 1), pad(period, 0), pad(bias, per_wide), pad(first, 1),
            pad(last, 0), pad(dist, 0)), len(tile)


def _attention(q, kb, vb, slopes, lq, sg, lam_init):
    b, t, width = q.shape
    n_heads = width // LANES
    tq = min(ATTN_TQ, t)
    wide = min(ALIBI_PERIOD, t)
    assert t % wide == 0 and wide % tq == 0
    per_wide = wide // tq
    eqt, ek = _alibi_rows_cols(slopes, wide)
    eqt = eqt.reshape(n_heads, LANES, per_wide, tq).transpose(0, 2, 1, 3)
    tables, n_steps = _attention_schedule(t // tq, per_wide)
    ones_rows = 16
    n_prefetch = 1 + len(tables)
    head_spec = pl.BlockSpec((1, t, LANES), lambda bi, h, *_: (bi, 0, h))
    grid_spec = pltpu.PrefetchScalarGridSpec(
        num_scalar_prefetch=n_prefetch,
        grid=(b, n_heads),
        in_specs=[head_spec, head_spec, head_spec,
                  pl.BlockSpec((1, per_wide, LANES, tq), lambda bi, h, *_: (h, 0, 0, 0)),
                  pl.BlockSpec((1, wide, LANES), lambda bi, h, *_: (h, 0, 0)),
                  pl.BlockSpec(lq.shape, lambda bi, h, *_: (0, 0)),
                  pl.BlockSpec(sg.shape, lambda bi, h, *_: (0, 0))],
        out_specs=head_spec,
        scratch_shapes=[pltpu.VMEM((t, 2 * LANES), BF16),
                        pltpu.VMEM((LANES + ones_rows, t), BF16),
                        pltpu.VMEM((per_wide, wide, tq), F32),
                        pltpu.VMEM((2, 2 * LANES, tq), BF16),
                        pltpu.VMEM((2, 2, wide, tq), F32),
                        pltpu.VMEM((2, MXU_DIM, tq), BF16),
                        pltpu.VMEM((2, 2, 8, tq), F32),
                        pltpu.VMEM((2, 8, tq), F32),
                        pltpu.VMEM((2, LANES + ones_rows, tq), F32),
                        pltpu.VMEM((2, LANES + ones_rows, tq), F32)],
    )
    return pl.pallas_call(
        functools.partial(_attn_kernel, lam_init=lam_init, wide=wide, tq=tq, n_steps=n_steps),
        grid_spec=grid_spec,
        out_shape=jax.ShapeDtypeStruct((b, t, width), BF16),
        compiler_params=_params(2),
        name="diff_attention",
    )(jnp.asarray(slopes, F32), *tables, q, kb, vb, eqt, ek, lq, sg)


def _attn_cache_kernel(slope_ref, q_ref, kn_ref, vn_ref, ck_ref, cv_ref, lq_ref, sg_ref, o_ref, *,
                       lam_init):
    h = pl.program_id(1)
    slope = slope_ref[h]
    tn = q_ref.shape[1]
    past = ck_ref.shape[1]
    q1, q2 = _split_maps(q_ref[0])
    ck = ck_ref[0].astype(BF16)
    cv = cv_ref[0].astype(BF16)
    kn = kn_ref[0]
    vn = vn_ref[0]

    def bias_for(k0, width):
        qpos = past + lax.broadcasted_iota(jnp.int32, (tn, width), 0)
        kpos = k0 + lax.broadcasted_iota(jnp.int32, (tn, width), 1)
        visible = (kpos // CHUNK) <= (qpos // CHUNK)
        return jnp.where(visible, -slope * jnp.abs(qpos - kpos).astype(F32), NEG_BIG)

    bias_c = bias_for(0, past)
    bias_n = bias_for(past, tn)

    def one_map(a):
        sc = _qk(a, ck) + bias_c
        sn = _qk(a, kn) + bias_n
        m = jnp.maximum(jnp.max(sc, axis=-1, keepdims=True), jnp.max(sn, axis=-1, keepdims=True))
        pc = jnp.exp2(sc - m)
        pn = jnp.exp2(sn - m)
        l = jnp.sum(pc, axis=-1, keepdims=True) + jnp.sum(pn, axis=-1, keepdims=True)
        acc = (jnp.dot(pc.astype(BF16), cv, preferred_element_type=F32)
               + jnp.dot(pn.astype(BF16), vn, preferred_element_type=F32))
        return l, acc

    l1, acc1 = one_map(q1)
    l2, acc2 = one_map(q2)
    lam = _lambda(lq_ref, lam_init)
    o_ref[0] = _finish_heads(acc1, l1, acc2, l2, lam, sg_ref[...], lam_init).astype(o_ref.dtype)


def _attention_cached(q, kb, vb, cache_k, cache_v, slopes, lq, sg, lam_init):
    b, tn, width = q.shape
    past = cache_k.shape[1]
    n_heads = width // LANES
    new_spec = pl.BlockSpec((1, tn, LANES), lambda bi, h, s: (bi, 0, h))
    past_spec = pl.BlockSpec((1, past, LANES), lambda bi, h, s: (bi, 0, h))
    grid_spec = pltpu.PrefetchScalarGridSpec(
        num_scalar_prefetch=1,
        grid=(b, n_heads),
        in_specs=[new_spec, new_spec, new_spec, past_spec, past_spec,
                  pl.BlockSpec(lq.shape, lambda bi, h, s: (0, 0)),
                  pl.BlockSpec(sg.shape, lambda bi, h, s: (0, 0))],
        out_specs=new_spec,
    )
    return pl.pallas_call(
        functools.partial(_attn_cache_kernel, lam_init=lam_init),
        grid_spec=grid_spec,
        out_shape=jax.ShapeDtypeStruct((b, tn, width), BF16),
        compiler_params=_params(2),
        name="diff_attention_cached",
    )(jnp.asarray(slopes, F32), q, kb, vb, cache_k, cache_v, lq, sg)


def _out_kernel(x_ref, o_ref, ga_ref, gb_ref, mods_ref, ng_ref, wba_ref, wout_ref, wgu_ref, wd_ref,
                y_ref):
    x = x_ref[0]
    oa = jnp.dot(o_ref[0], wba_ref[...], preferred_element_type=F32)
    mixed = (ga_ref[0].astype(F32) * oa + gb_ref[0].astype(F32)).astype(BF16)
    x = x + mods_ref[0, 5:6, :] * jnp.dot(mixed, wout_ref[...], preferred_element_type=F32)
    h = _mod_norm(x, ng_ref[2:3, :], mods_ref, 6).astype(BF16)
    x = x + 0.5 * mods_ref[0, 8:9, :] * _swiglu(h, wgu_ref, wd_ref)
    y_ref[0] = _rms(x) * ng_ref[3:4, :]


def _mix_ffn2(x, o, ga, gb, mods, mods_off, norm_g, wba, wout, wgu, wd, tm):
    b, t, d = x.shape
    row_spec = pl.BlockSpec((1, tm, d), lambda bi, i: (bi, i, 0))
    consts = (norm_g, wba, wout, wgu, wd)
    return pl.pallas_call(
        _out_kernel,
        grid=(b, t // tm),
        in_specs=[row_spec] * 4 + [pl.BlockSpec((1, N_MODS, d), lambda bi, i: (bi + mods_off, 0, 0))]
        + [_const_spec(a.shape) for a in consts],
        out_specs=row_spec,
        out_shape=jax.ShapeDtypeStruct((b, t, d), F32),
        compiler_params=_params(2),
        name="mix_ffn2",
    )(x, o, ga, gb, mods, *consts)


def _row_tile(t, target):
    tm = min(t, target)
    while t % tm:
        tm //= 2
    return tm


def kernel(x_prompt, x_sample, cache_k, cache_v, c_prompt, c_sample, ada_w, ada_b, norm_g, ffn1_wgu,
           ffn1_wd, w_in, q_norm_g, k_norm_g, lambda_qk, attn_subln_g, gmlp_vnorm_g, gmlp_ws, gmlp_bs,
           w_gate, b_gate, w_branch, w_out, ffn2_wgu, ffn2_wd):
    assert ada_w.shape[0] == 1, "single-layer step"
    bp, tp, d = x_prompt.shape
    bs_, ts, _ = x_sample.shape
    _, _, past, n_heads, _, hd = cache_k.shape
    assert n_heads == 8 and 2 * hd == LANES and n_heads * LANES == d
    lam_init = _lambda_init(0)
    log2e = math.log2(math.e)
    slopes = (2.0 ** (-8.0 * np.arange(1, n_heads + 1) / n_heads) * log2e).astype(np.float32)
    q_scale = float(hd) ** -0.5 * log2e

    assert ffn1_wd.shape[1] % FFN_CHUNK == 0 and w_in.shape[2] == 5 * d
    wgu1, wd1 = ffn1_wgu[0].astype(BF16), ffn1_wd[0].astype(BF16)
    wgu2, wd2 = ffn2_wgu[0].astype(BF16), ffn2_wd[0].astype(BF16)
    win = w_in[0].astype(BF16)
    wgate = w_gate[0].astype(BF16)
    bgate = b_gate[0].reshape(2, d)
    wba = w_branch[0, :d].astype(BF16)
    wbb = w_branch[0, d:].astype(BF16)
    wout = w_out[0].astype(BF16)
    ng = norm_g[0]
    qg = jnp.tile(q_norm_g[0], d // hd).reshape(1, d)
    kg = jnp.tile(k_norm_g[0], d // hd).reshape(1, d)
    grp = np.arange(d) // hd
    gmat = jnp.asarray((grp[:, None] == grp[None, :]).astype(np.float32) / hd, BF16)
    vng = gmlp_vnorm_g[0].reshape(1, d)
    sg = attn_subln_g[0].reshape(1, LANES)
    lq = lambda_qk[0]

    mods = _mods(jnp.concatenate([c_prompt, c_sample], axis=0), ada_w[0], ada_b[0])
    mods = mods.reshape(bp + bs_, N_MODS, d)

    def gmlp_consts(t):
        rows = min(t, GMLP_CHUNK)
        ws = gmlp_ws[0][:, :rows, :rows]
        bs = jnp.repeat(gmlp_bs[0][:, :rows].T, d // GMLP_GROUPS, axis=1)
        return rows, ws, bs

    def layer(x, mods_off, attend, tm_ffn, tm_proj, want_gv):
        b, t, _ = x.shape
        rows, ws, bs = gmlp_consts(t)
        x1 = _ffn1(x, mods, mods_off, ng, wgu1, wd1, _row_tile(t, tm_ffn))
        tmp = max(_row_tile(t, tm_proj), rows)
        outs = _proj(x1, mods, mods_off, ng, win, wgate, bgate, qg, kg, gmat, vng, ws, bs, wbb,
                     tmp, rows, q_scale, want_gv)
        q, k, v, kb, vb, ga, gb = outs[:7]
        o = attend(q, kb, vb)
        y = _mix_ffn2(x1, o, ga, gb, mods, mods_off, ng, wba, wout, wgu2, wd2,
                      _row_tile(t, tm_ffn))
        return y, k, v, (outs[7] if want_gv else None)

    y_p, k_p, v_p, _ = layer(
        x_prompt, 0, lambda q, kb, vb: _attention(q, kb, vb, slopes, lq, sg, lam_init),
        512, 512, False)

    ck = cache_k[0].reshape(bs_, past, d)
    cv = cache_v[0].reshape(bs_, past, d)
    y_s, k_s, v_s, gv_s = layer(
        x_sample, bp,
        lambda q, kb, vb: _attention_cached(q, kb, vb, ck, cv, slopes, lq, sg, lam_init),
        512, 256, True)

    return (y_p, y_s,
            k_p.reshape(1, bp, tp, n_heads, 2, hd), v_p.reshape(1, bp, tp, n_heads, 2 * hd),
            k_s.reshape(1, bs_, ts, n_heads, 2, hd), v_s.reshape(1, bs_, ts, n_heads, 2 * hd),
            gv_s.reshape(1, bs_, ts, d))
```

```python
import functools
import math

import numpy as np
import jax
import jax.numpy as jnp
from jax import lax
from jax.experimental import pallas as pl
from jax.experimental.pallas import tpu as pltpu

F32 = jnp.float32
BF16 = jnp.bfloat16

EPS = 1e-6
CHUNK = 64
GMLP_CHUNK = 128
GMLP_GROUPS = 8
N_MODS = 9
NEG_BIG = -1e30

V7X_VMEM_BYTES = 64 * 1024 * 1024
VMEM_LIMIT_BYTES = V7X_VMEM_BYTES - 8 * 1024 * 1024
LANES = 128
MXU_DIM = 256
ATTN_TQ = 512
FFN_CHUNK = 256
ALIBI_SPLIT = 256
ALIBI_PERIOD = 1024


def _lambda_init(layer_idx):
    return 0.8 - 0.6 * math.exp(-0.3 * layer_idx)


def _const_spec(shape):
    nd = len(shape)
    return pl.BlockSpec(shape, lambda *_: (0,) * nd, pipeline_mode=pl.Buffered(1))


def _params(n_axes, flags=None):
    return pltpu.CompilerParams(
        dimension_semantics=("arbitrary",) * n_axes, vmem_limit_bytes=VMEM_LIMIT_BYTES, flags=flags)


def _mods_kernel(c_ref, w_ref, b_ref, o_ref):
    c = c_ref[...]
    h = c * jax.nn.sigmoid(c)
    o_ref[...] = jnp.dot(h, w_ref[...], preferred_element_type=F32,
                         precision=lax.Precision.HIGHEST) + b_ref[...]


def _mods(c_all, ada_w, ada_b):
    n, d = c_all.shape
    width = ada_w.shape[1]
    tn = d
    return pl.pallas_call(
        _mods_kernel,
        grid=(width // tn,),
        in_specs=[pl.BlockSpec((n, d), lambda j: (0, 0)),
                  pl.BlockSpec((d, tn), lambda j: (0, j)),
                  pl.BlockSpec((1, tn), lambda j: (0, j))],
        out_specs=pl.BlockSpec((n, tn), lambda j: (0, j)),
        out_shape=jax.ShapeDtypeStruct((n, width), F32),
        compiler_params=_params(1),
        name="adaln_mods",
    )(c_all, ada_w, ada_b.reshape(1, width))


def _rms(x):
    return x * lax.rsqrt(jnp.mean(x * x, axis=-1, keepdims=True) + EPS)


def _mod_norm(x, g, mods_ref, first):
    sh = mods_ref[0, first:first + 1, :]
    sc = mods_ref[0, first + 1:first + 2, :]
    return (_rms(x) * g) * (1.0 + sc) + sh


def _swiglu(h, wgu_ref, wd_ref):
    f = wd_ref.shape[0]
    acc = None
    for c0 in range(0, f, FFN_CHUNK):
        gate = jnp.dot(h, wgu_ref[:, c0:c0 + FFN_CHUNK], preferred_element_type=F32)
        up = jnp.dot(h, wgu_ref[:, f + c0:f + c0 + FFN_CHUNK], preferred_element_type=F32)
        a = (gate * jax.nn.sigmoid(gate) * up).astype(BF16)
        part = jnp.dot(a, wd_ref[c0:c0 + FFN_CHUNK, :], preferred_element_type=F32)
        acc = part if acc is None else acc + part
    return acc


def _gelu_tanh(x):
    return 0.5 * x * (1.0 + jnp.tanh(math.sqrt(2.0 / math.pi) * (x + 0.044715 * (x * x * x))))


def _ffn_kernel(x_ref, mods_ref, ng_ref, wgu_ref, wd_ref, o_ref):
    x = x_ref[0]
    h = _mod_norm(x, ng_ref[0:1, :], mods_ref, 0).astype(BF16)
    gt = mods_ref[0, 2:3, :]
    o_ref[0] = x + 0.5 * gt * _swiglu(h, wgu_ref, wd_ref)


def _ffn1(x, mods, mods_off, norm_g, wgu, wd, tm):
    b, t, d = x.shape
    return pl.pallas_call(
        _ffn_kernel,
        grid=(b, t // tm),
        in_specs=[pl.BlockSpec((1, tm, d), lambda bi, i: (bi, i, 0)),
                  pl.BlockSpec((1, N_MODS, d), lambda bi, i: (bi + mods_off, 0, 0)),
                  _const_spec(norm_g.shape), _const_spec(wgu.shape), _const_spec(wd.shape)],
        out_specs=pl.BlockSpec((1, tm, d), lambda bi, i: (bi, i, 0)),
        out_shape=jax.ShapeDtypeStruct((b, t, d), F32),
        compiler_params=_params(2),
        name="ffn1",
    )(x, mods, norm_g, wgu, wd)


def _proj_kernel(x_ref, mods_ref, ng_ref, win_ref, wgate_ref, bgate_ref, qg_ref, kg_ref, gmat_ref,
                 vng_ref, ws_ref, bs_ref, wbb_ref,
                 q_o, k_o, v_o, kb_o, vb_o, ga_o, gb_o, *maybe_gv_o, rows, q_scale):
    x = x_ref[0]
    tm, d = x.shape
    h = _mod_norm(x, ng_ref[1:2, :], mods_ref, 3).astype(BF16)

    def group_norm(z, g):
        ms = jnp.dot((z * z).astype(BF16), gmat_ref[...], preferred_element_type=F32)
        return z * lax.rsqrt(ms + EPS) * g

    def w_in(j):
        return win_ref[:, j * d:(j + 1) * d]

    q = group_norm(jnp.dot(h, w_in(0), preferred_element_type=F32), qg_ref[...])
    q_o[0] = (q * q_scale).astype(BF16)
    k = group_norm(jnp.dot(h, w_in(1), preferred_element_type=F32), kg_ref[...])
    k_o[0] = k
    kb_o[0] = k.astype(BF16)
    v = jnp.dot(h, w_in(2), preferred_element_type=F32)
    v_o[0] = v
    vb_o[0] = v.astype(BF16)

    u = _gelu_tanh(jnp.dot(h, w_in(3), preferred_element_type=F32))
    gz = _gelu_tanh(jnp.dot(h, w_in(4), preferred_element_type=F32))
    gv = _rms(gz) * vng_ref[...]
    if maybe_gv_o:
        maybe_gv_o[0][0] = gv
    gvb = gv.astype(BF16)

    row_i = lax.broadcasted_iota(jnp.int32, (rows, rows), 0)
    col_i = lax.broadcasted_iota(jnp.int32, (rows, rows), 1)
    tril = col_i <= row_i
    gw = d // GMLP_GROUPS
    ws = [jnp.where(tril, ws_ref[g], 0.0).astype(BF16) for g in range(GMLP_GROUPS)]
    s_chunks = []
    for c in range(tm // rows):
        r0 = c * rows
        mix = jnp.concatenate(
            [jnp.dot(ws[g], gvb[r0:r0 + rows, g * gw:(g + 1) * gw], preferred_element_type=F32)
             for g in range(GMLP_GROUPS)], axis=1)
        s_chunks.append(u[r0:r0 + rows, :] * (mix + bs_ref[...]))
    s = s_chunks[0] if len(s_chunks) == 1 else jnp.concatenate(s_chunks, axis=0)
    sb = jnp.dot(s.astype(BF16), wbb_ref[...], preferred_element_type=F32)

    gate_a = jax.nn.sigmoid(
        jnp.dot(h, wgate_ref[:, 0:d], preferred_element_type=F32) + bgate_ref[0:1, :])
    gate_b = jax.nn.sigmoid(
        jnp.dot(h, wgate_ref[:, d:2 * d], preferred_element_type=F32) + bgate_ref[1:2, :])
    ga_o[0] = gate_a.astype(BF16)
    gb_o[0] = (gate_b * sb).astype(BF16)


def _proj(x, mods, mods_off, norm_g, win, wgate, bgate, qg, kg, gmat, vng, ws, bs, wbb, tm, rows,
          q_scale, want_gv):
    b, t, d = x.shape
    row_spec = pl.BlockSpec((1, tm, d), lambda bi, i: (bi, i, 0))
    out_shapes = [jax.ShapeDtypeStruct((b, t, d), BF16),
                  jax.ShapeDtypeStruct((b, t, d), F32),
                  jax.ShapeDtypeStruct((b, t, d), F32),
                  jax.ShapeDtypeStruct((b, t, d), BF16),
                  jax.ShapeDtypeStruct((b, t, d), BF16),
                  jax.ShapeDtypeStruct((b, t, d), BF16),
                  jax.ShapeDtypeStruct((b, t, d), BF16)]
    if want_gv:
        out_shapes.append(jax.ShapeDtypeStruct((b, t, d), F32))
    consts = (norm_g, win, wgate, bgate, qg, kg, gmat, vng, ws, bs, wbb)
    return pl.pallas_call(
        functools.partial(_proj_kernel, rows=rows, q_scale=q_scale),
        grid=(b, t // tm),
        in_specs=[row_spec, pl.BlockSpec((1, N_MODS, d), lambda bi, i: (bi + mods_off, 0, 0))]
        + [_const_spec(a.shape) for a in consts],
        out_specs=[row_spec] * len(out_shapes),
        out_shape=out_shapes,
        compiler_params=_params(2),
        name="mixer_proj",
    )(x, mods, *consts)


def _split_maps(q):
    lane = lax.broadcasted_iota(jnp.int32, q.shape, 1)
    half = q.shape[1] // 2
    zero = jnp.zeros_like(q)
    return jnp.where(lane < half, q, zero), jnp.where(lane >= half, q, zero)


def _qk(a, k):
    return lax.dot_general(a, k, (((1,), (1,)), ((), ())), preferred_element_type=F32)


def _lambda(lq_ref, lam_init):
    lq = lq_ref[...]
    t1 = jnp.sum(lq[0:1, :] * lq[1:2, :], axis=-1, keepdims=True)
    t2 = jnp.sum(lq[2:3, :] * lq[3:4, :], axis=-1, keepdims=True)
    return jnp.exp(t1) - jnp.exp(t2) + lam_init


def _finish_heads(acc1, l1, acc2, l2, lam, sg, lam_init):
    o = acc1 / l1 - lam * (acc2 / l2)
    return (_rms(o) * sg) * (1.0 - lam_init)


def _attn_kernel(slope_ref, tile_ref, period_ref, bias_ref, first_ref, last_ref, dist_ref,
                 q_ref, k_ref, v_ref, eqt_ref, ek_ref, lq_ref, sg_ref, o_ref,
                 kx_scr, vt_scr, bias_scr, at_scr, s_scr, p_last_scr, smax_scr, m_scr, fin_scr,
                 acc_scr, *, lam_init, wide, tq, n_steps):
    h = pl.program_id(1)
    t_all = k_ref.shape[1]
    slope = slope_ref[h]
    per_wide = wide // tq

    kx_scr[:, 0:LANES] = k_ref[0]
    for c in range(t_all // wide):
        kx_scr[c * wide:(c + 1) * wide, LANES:2 * LANES] = ek_ref[0]
        vt_scr[0:LANES, c * wide:(c + 1) * wide] = (
            v_ref[0, c * wide:(c + 1) * wide, :].astype(F32).T.astype(BF16))
    vt_scr[LANES:, :] = jnp.ones((vt_scr.shape[0] - LANES, t_all), BF16)
    s_pos = lax.broadcasted_iota(jnp.int32, (wide, tq), 0)
    for c in range(per_wide):
        t_pos = c * tq + lax.broadcasted_iota(jnp.int32, (wide, tq), 1)
        visible = (s_pos // CHUNK) <= (t_pos // CHUNK)
        fix = (jnp.abs(t_pos - s_pos) - (t_pos - s_pos)).astype(F32)
        bias_scr[c] = jnp.where(visible, -slope * fix, NEG_BIG)
    p_last_scr[...] = jnp.zeros(p_last_scr.shape, BF16)
    acc_scr[...] = jnp.zeros(acc_scr.shape, F32)
    m_scr[...] = jnp.full(m_scr.shape, NEG_BIG, F32)

    def build_queries(tile):
        r0 = pl.multiple_of(tile * tq, tq)
        qt = q_ref[0, pl.ds(r0, tq), :].astype(F32).T
        row = lax.broadcasted_iota(jnp.int32, qt.shape, 0)
        zero = jnp.zeros_like(qt)
        buf = tile % 2
        at_scr[buf, 0, 0:LANES, :] = jnp.where(row < LANES // 2, qt, zero).astype(BF16)
        at_scr[buf, 1, 0:LANES, :] = jnp.where(row >= LANES // 2, qt, zero).astype(BF16)
        alibi_rows = eqt_ref[0, tile % per_wide]
        at_scr[buf, 0, LANES:2 * LANES, :] = alibi_rows
        at_scr[buf, 1, LANES:2 * LANES, :] = alibi_rows

    sub = MXU_DIM
    n_sub = wide // sub

    def scores(e, slot, masked):
        kx = kx_scr[pl.ds(pl.multiple_of(period_ref[e] * wide, wide), wide), :]
        buf = tile_ref[e] % 2
        for mp in range(2):
            s = jnp.dot(kx, at_scr[buf, mp], preferred_element_type=F32)
            if masked:
                s = s + bias_scr[bias_ref[e]]
            s_scr[slot, mp] = s
            smax_scr[slot, mp, 0:1, :] = jnp.max(s, axis=0, keepdims=True)

    def values_block(e, c):
        off = pl.multiple_of(period_ref[e] * wide + c * sub, sub)
        return vt_scr[:, pl.ds(off, sub)]

    def pipeline_step(e, slot, masked):
        shift = -slope * (dist_ref[e] * wide).astype(F32)
        is_first = first_ref[e] == 1
        offsets, alphas = [], []
        for mp in range(2):
            m_old = jnp.where(is_first, NEG_BIG, m_scr[mp, 0:1, :])
            m_new = jnp.maximum(m_old, smax_scr[slot, mp, 0:1, :] + shift)
            alphas.append(jnp.exp2(m_old - m_new))
            m_scr[mp, 0:1, :] = m_new
            offsets.append(m_new - shift)
        vt_pending = values_block(e - 1, n_sub - 1)
        pending = [jnp.dot(vt_pending, p_last_scr[mp], preferred_element_type=F32) for mp in range(2)]
        scores(e + 1, 1 - slot, masked)
        pv = [None, None]
        for c in range(n_sub):
            rows = slice(c * sub, (c + 1) * sub)
            for mp in range(2):
                p = jnp.exp2(s_scr[slot, mp, rows, :] - offsets[mp]).astype(BF16)
                if c == n_sub - 1:
                    p_last_scr[mp] = p
                else:
                    part = jnp.dot(values_block(e, c), p, preferred_element_type=F32)
                    pv[mp] = part if pv[mp] is None else pv[mp] + part
        for mp in range(2):
            done = acc_scr[mp] + pending[mp]
            fin_scr[slot, mp] = done
            acc_scr[mp] = alphas[mp] * done + pv[mp]

    def finalize(tile, slot):
        lam = _lambda(lq_ref, lam_init)
        a1 = fin_scr[slot, 0]
        a2 = fin_scr[slot, 1]
        ot = a1[0:LANES] / a1[LANES:LANES + 1] - lam * (a2[0:LANES] / a2[LANES:LANES + 1])
        ot = ot * lax.rsqrt(jnp.mean(ot * ot, axis=0, keepdims=True) + EPS)
        r0 = pl.multiple_of(tile * tq, tq)
        o_ref[0, pl.ds(r0, tq), :] = ((ot.T * sg_ref[...]) * (1.0 - lam_init)).astype(o_ref.dtype)

    build_queries(tile_ref[1])
    scores(1, 0, True)

    def iteration(kk, carry):
        e = 2 * kk + 1

        for nxt in (e + 1, e + 2):
            @pl.when(first_ref[nxt] == 1)
            def _():
                build_queries(tile_ref[nxt])

        masked = [bias_ref[e + 1] < per_wide, bias_ref[e + 2] < per_wide]
        for pattern in ((False, False), (False, True), (True, False), (True, True)):
            conds = [m if want else jnp.logical_not(m) for m, want in zip(masked, pattern)]

            @pl.when(jnp.logical_and(conds[0], conds[1]))
            def _():
                pipeline_step(e, 0, pattern[0])
                pipeline_step(e + 1, 1, pattern[1])

        for slot in (0, 1):
            @pl.when(last_ref[e - 1 + slot] == 1)
            def _():
                finalize(tile_ref[e - 1 + slot], slot)

        return carry

    lax.fori_loop(0, n_steps // 2 + 1, iteration, 0)


def _bf16_pieces(x, n):
    pieces, rest = [], np.asarray(x, np.float32)
    for _ in range(n):
        p = rest.astype(BF16).astype(np.float32)
        pieces.append(p)
        rest = rest - p
    return pieces


def _alibi_rows_cols(slopes, period):
    n_heads = slopes.shape[0]
    pos = np.arange(period)
    offs = ((pos // ALIBI_SPLIT) * ALIBI_SPLIT, pos % ALIBI_SPLIT)
    eqt = np.zeros((n_heads, LANES, period), np.float32)
    ek = np.zeros((n_heads, period, LANES), np.float32)
    lane = 0
    for piece in _bf16_pieces(slopes, 3):
        for off in offs:
            eqt[:, lane, :] = off[None, :]
            ek[:, :, lane] = -piece[:, None]
            eqt[:, lane + 1, :] = piece[:, None]
            ek[:, :, lane + 1] = off[None, :]
            lane += 2
    return jnp.asarray(eqt, BF16), jnp.asarray(ek, BF16)


def _attention_schedule(n_tiles, per_wide):
    tile, period, bias, first, last, dist = [], [], [], [], [], []
    for i in range(n_tiles):
        n_wide = i // per_wide
        for j in range(n_wide + 1):
            own = j == n_wide
            tile.append(i)
            period.append(j)
            bias.append(i % per_wide if own else per_wide)
            first.append(int(j == 0))
            last.append(int(own))
            dist.append(n_wide - j)
    pad = lambda v, fill: np.asarray([fill] + v + [fill] * 3, np.int32)
    return (pad(tile, n_tiles - 1), pad(period, 0), pad(bias, per_wide), pad(first, 1),
            pad(last, 0), pad(dist, 0)), len(tile)


def _attention(q, kb, vb, slopes, lq, sg, lam_init):
    b, t, width = q.shape
    n_heads = width // LANES
    tq = min(ATTN_TQ, t)
    wide = min(ALIBI_PERIOD, t)
    assert t % wide == 0 and wide % tq == 0
    per_wide = wide // tq
    eqt, ek = _alibi_rows_cols(slopes, wide)
    eqt = eqt.reshape(n_heads, LANES, per_wide, tq).transpose(0, 2, 1, 3)
    tables, n_steps = _attention_schedule(t // tq, per_wide)
    ones_rows = 16
    n_prefetch = 1 + len(tables)
    head_spec = pl.BlockSpec((1, t, LANES), lambda bi, h, *_: (bi, 0, h))
    grid_spec = pltpu.PrefetchScalarGridSpec(
        num_scalar_prefetch=n_prefetch,
        grid=(b, n_heads),
        in_specs=[head_spec, head_spec, head_spec,
                  pl.BlockSpec((1, per_wide, LANES, tq), lambda bi, h, *_: (h, 0, 0, 0)),
                  pl.BlockSpec((1, wide, LANES), lambda bi, h, *_: (h, 0, 0)),
                  pl.BlockSpec(lq.shape, lambda bi, h, *_: (0, 0)),
                  pl.BlockSpec(sg.shape, lambda bi, h, *_: (0, 0))],
        out_specs=head_spec,
        scratch_shapes=[pltpu.VMEM((t, 2 * LANES), BF16),
                        pltpu.VMEM((LANES + ones_rows, t), BF16),
                        pltpu.VMEM((per_wide, wide, tq), F32),
                        pltpu.VMEM((2, 2, 2 * LANES, tq), BF16),
                        pltpu.VMEM((2, 2, wide, tq), F32),
                        pltpu.VMEM((2, MXU_DIM, tq), BF16),
                        pltpu.VMEM((2, 2, 8, tq), F32),
                        pltpu.VMEM((2, 8, tq), F32),
                        pltpu.VMEM((2, 2, LANES + ones_rows, tq), F32),
                        pltpu.VMEM((2, LANES + ones_rows, tq), F32)],
    )
    return pl.pallas_call(
        functools.partial(_attn_kernel, lam_init=lam_init, wide=wide, tq=tq, n_steps=n_steps),
        grid_spec=grid_spec,
        out_shape=jax.ShapeDtypeStruct((b, t, width), BF16),
        compiler_params=_params(2),
        name="diff_attention",
    )(jnp.asarray(slopes, F32), *tables, q, kb, vb, eqt, ek, lq, sg)


def _attn_cache_kernel(slope_ref, q_ref, kn_ref, vn_ref, ck_ref, cv_ref, lq_ref, sg_ref, o_ref, *,
                       lam_init):
    h = pl.program_id(1)
    slope = slope_ref[h]
    tn = q_ref.shape[1]
    past = ck_ref.shape[1]
    q1, q2 = _split_maps(q_ref[0])
    ck = ck_ref[0].astype(BF16)
    cv = cv_ref[0].astype(BF16)
    kn = kn_ref[0]
    vn = vn_ref[0]

    def bias_for(k0, width):
        qpos = past + lax.broadcasted_iota(jnp.int32, (tn, width), 0)
        kpos = k0 + lax.broadcasted_iota(jnp.int32, (tn, width), 1)
        visible = (kpos // CHUNK) <= (qpos // CHUNK)
        return jnp.where(visible, -slope * jnp.abs(qpos - kpos).astype(F32), NEG_BIG)

    bias_c = bias_for(0, past)
    bias_n = bias_for(past, tn)

    def one_map(a):
        sc = _qk(a, ck) + bias_c
        sn = _qk(a, kn) + bias_n
        m = jnp.maximum(jnp.max(sc, axis=-1, keepdims=True), jnp.max(sn, axis=-1, keepdims=True))
        pc = jnp.exp2(sc - m)
        pn = jnp.exp2(sn - m)
        l = jnp.sum(pc, axis=-1, keepdims=True) + jnp.sum(pn, axis=-1, keepdims=True)
        acc = (jnp.dot(pc.astype(BF16), cv, preferred_element_type=F32)
               + jnp.dot(pn.astype(BF16), vn, preferred_element_type=F32))
        return l, acc

    l1, acc1 = one_map(q1)
    l2, acc2 = one_map(q2)
    lam = _lambda(lq_ref, lam_init)
    o_ref[0] = _finish_heads(acc1, l1, acc2, l2, lam, sg_ref[...], lam_init).astype(o_ref.dtype)


def _attention_cached(q, kb, vb, cache_k, cache_v, slopes, lq, sg, lam_init):
    b, tn, width = q.shape
    past = cache_k.shape[1]
    n_heads = width // LANES
    new_spec = pl.BlockSpec((1, tn, LANES), lambda bi, h, s: (bi, 0, h))
    past_spec = pl.BlockSpec((1, past, LANES), lambda bi, h, s: (bi, 0, h))
    grid_spec = pltpu.PrefetchScalarGridSpec(
        num_scalar_prefetch=1,
        grid=(b, n_heads),
        in_specs=[new_spec, new_spec, new_spec, past_spec, past_spec,
                  pl.BlockSpec(lq.shape, lambda bi, h, s: (0, 0)),
                  pl.BlockSpec(sg.shape, lambda bi, h, s: (0, 0))],
        out_specs=new_spec,
    )
    return pl.pallas_call(
        functools.partial(_attn_cache_kernel, lam_init=lam_init),
        grid_spec=grid_spec,
        out_shape=jax.ShapeDtypeStruct((b, tn, width), BF16),
        compiler_params=_params(2),
        name="diff_attention_cached",
    )(jnp.asarray(slopes, F32), q, kb, vb, cache_k, cache_v, lq, sg)


def _out_kernel(x_ref, o_ref, ga_ref, gb_ref, mods_ref, ng_ref, wba_ref, wout_ref, wgu_ref, wd_ref,
                y_ref):
    x = x_ref[0]
    oa = jnp.dot(o_ref[0], wba_ref[...], preferred_element_type=F32)
    mixed = (ga_ref[0].astype(F32) * oa + gb_ref[0].astype(F32)).astype(BF16)
    x = x + mods_ref[0, 5:6, :] * jnp.dot(mixed, wout_ref[...], preferred_element_type=F32)
    h = _mod_norm(x, ng_ref[2:3, :], mods_ref, 6).astype(BF16)
    x = x + 0.5 * mods_ref[0, 8:9, :] * _swiglu(h, wgu_ref, wd_ref)
    y_ref[0] = _rms(x) * ng_ref[3:4, :]


def _mix_ffn2(x, o, ga, gb, mods, mods_off, norm_g, wba, wout, wgu, wd, tm):
    b, t, d = x.shape
    row_spec = pl.BlockSpec((1, tm, d), lambda bi, i: (bi, i, 0))
    consts = (norm_g, wba, wout, wgu, wd)
    return pl.pallas_call(
        _out_kernel,
        grid=(b, t // tm),
        in_specs=[row_spec] * 4 + [pl.BlockSpec((1, N_MODS, d), lambda bi, i: (bi + mods_off, 0, 0))]
        + [_const_spec(a.shape) for a in consts],
        out_specs=row_spec,
        out_shape=jax.ShapeDtypeStruct((b, t, d), F32),
        compiler_params=_params(2),
        name="mix_ffn2",
    )(x, o, ga, gb, mods, *consts)


def _row_tile(t, target):
    tm = min(t, target)
    while t % tm:
        tm //= 2
    return tm


def kernel(x_prompt, x_sample, cache_k, cache_v, c_prompt, c_sample, ada_w, ada_b, norm_g, ffn1_wgu,
           ffn1_wd, w_in, q_norm_g, k_norm_g, lambda_qk, attn_subln_g, gmlp_vnorm_g, gmlp_ws, gmlp_bs,
           w_gate, b_gate, w_branch, w_out, ffn2_wgu, ffn2_wd):
    assert ada_w.shape[0] == 1, "single-layer step"
    bp, tp, d = x_prompt.shape
    bs_, ts, _ = x_sample.shape
    _, _, past, n_heads, _, hd = cache_k.shape
    assert n_heads == 8 and 2 * hd == LANES and n_heads * LANES == d
    lam_init = _lambda_init(0)
    log2e = math.log2(math.e)
    slopes = (2.0 ** (-8.0 * np.arange(1, n_heads + 1) / n_heads) * log2e).astype(np.float32)
    q_scale = float(hd) ** -0.5 * log2e

    assert ffn1_wd.shape[1] % FFN_CHUNK == 0 and w_in.shape[2] == 5 * d
    wgu1, wd1 = ffn1_wgu[0].astype(BF16), ffn1_wd[0].astype(BF16)
    wgu2, wd2 = ffn2_wgu[0].astype(BF16), ffn2_wd[0].astype(BF16)
    win = w_in[0].astype(BF16)
    wgate = w_gate[0].astype(BF16)
    bgate = b_gate[0].reshape(2, d)
    wba = w_branch[0, :d].astype(BF16)
    wbb = w_branch[0, d:].astype(BF16)
    wout = w_out[0].astype(BF16)
    ng = norm_g[0]
    qg = jnp.tile(q_norm_g[0], d // hd).reshape(1, d)
    kg = jnp.tile(k_norm_g[0], d // hd).reshape(1, d)
    grp = np.arange(d) // hd
    gmat = jnp.asarray((grp[:, None] == grp[None, :]).astype(np.float32) / hd, BF16)
    vng = gmlp_vnorm_g[0].reshape(1, d)
    sg = attn_subln_g[0].reshape(1, LANES)
    lq = lambda_qk[0]

    mods = _mods(jnp.concatenate([c_prompt, c_sample], axis=0), ada_w[0], ada_b[0])
    mods = mods.reshape(bp + bs_, N_MODS, d)

    def gmlp_consts(t):
        rows = min(t, GMLP_CHUNK)
        ws = gmlp_ws[0][:, :rows, :rows]
        bs = jnp.repeat(gmlp_bs[0][:, :rows].T, d // GMLP_GROUPS, axis=1)
        return rows, ws, bs

    def layer(x, mods_off, attend, tm_ffn, tm_proj, want_gv):
        b, t, _ = x.shape
        rows, ws, bs = gmlp_consts(t)
        x1 = _ffn1(x, mods, mods_off, ng, wgu1, wd1, _row_tile(t, tm_ffn))
        tmp = max(_row_tile(t, tm_proj), rows)
        outs = _proj(x1, mods, mods_off, ng, win, wgate, bgate, qg, kg, gmat, vng, ws, bs, wbb,
                     tmp, rows, q_scale, want_gv)
        q, k, v, kb, vb, ga, gb = outs[:7]
        o = attend(q, kb, vb)
        y = _mix_ffn2(x1, o, ga, gb, mods, mods_off, ng, wba, wout, wgu2, wd2,
                      _row_tile(t, tm_ffn))
        return y, k, v, (outs[7] if want_gv else None)

    y_p, k_p, v_p, _ = layer(
        x_prompt, 0, lambda q, kb, vb: _attention(q, kb, vb, slopes, lq, sg, lam_init),
        512, 512, False)

    ck = cache_k[0].reshape(bs_, past, d)
    cv = cache_v[0].reshape(bs_, past, d)
    y_s, k_s, v_s, gv_s = layer(
        x_sample, bp,
        lambda q, kb, vb: _attention_cached(q, kb, vb, ck, cv, slopes, lq, sg, lam_init),
        512, 256, True)

    return (y_p, y_s,
            k_p.reshape(1, bp, tp, n_heads, 2, hd), v_p.reshape(1, bp, tp, n_heads, 2 * hd),
            k_s.reshape(1, bs_, ts, n_heads, 2, hd), v_s.reshape(1, bs_, ts, n_heads, 2 * hd),
            gv_s.reshape(1, bs_, ts, d))
```

```python
import functools
import math

import numpy as np
import jax
import jax.numpy as jnp
from jax import lax
from jax.experimental import pallas as pl
from jax.experimental.pallas import tpu as pltpu

F32 = jnp.float32
BF16 = jnp.bfloat16

EPS = 1e-6
CHUNK = 64
GMLP_CHUNK = 128
GMLP_GROUPS = 8
N_MODS = 9
NEG_BIG = -1e30

V7X_VMEM_BYTES = 64 * 1024 * 1024
VMEM_LIMIT_BYTES = V7X_VMEM_BYTES - 8 * 1024 * 1024
LANES = 128
MXU_DIM = 256
ATTN_TQ = 512
FFN_CHUNK = 256
ALIBI_SPLIT = 256
ALIBI_PERIOD = 1024


def _lambda_init(layer_idx):
    return 0.8 - 0.6 * math.exp(-0.3 * layer_idx)


def _const_spec(shape):
    nd = len(shape)
    return pl.BlockSpec(shape, lambda *_: (0,) * nd, pipeline_mode=pl.Buffered(1))


def _params(n_axes, flags=None):
    return pltpu.CompilerParams(
        dimension_semantics=("arbitrary",) * n_axes, vmem_limit_bytes=VMEM_LIMIT_BYTES, flags=flags)


def _mods_kernel(c_ref, w_ref, b_ref, o_ref):
    c = c_ref[...]
    h = c * jax.nn.sigmoid(c)
    o_ref[...] = jnp.dot(h, w_ref[...], preferred_element_type=F32,
                         precision=lax.Precision.HIGHEST) + b_ref[...]


def _mods(c_all, ada_w, ada_b):
    n, d = c_all.shape
    width = ada_w.shape[1]
    tn = d
    return pl.pallas_call(
        _mods_kernel,
        grid=(width // tn,),
        in_specs=[pl.BlockSpec((n, d), lambda j: (0, 0)),
                  pl.BlockSpec((d, tn), lambda j: (0, j)),
                  pl.BlockSpec((1, tn), lambda j: (0, j))],
        out_specs=pl.BlockSpec((n, tn), lambda j: (0, j)),
        out_shape=jax.ShapeDtypeStruct((n, width), F32),
        compiler_params=_params(1),
        name="adaln_mods",
    )(c_all, ada_w, ada_b.reshape(1, width))


def _rms(x):
    return x * lax.rsqrt(jnp.mean(x * x, axis=-1, keepdims=True) + EPS)


def _mod_norm(x, g, mods_ref, first):
    sh = mods_ref[0, first:first + 1, :]
    sc = mods_ref[0, first + 1:first + 2, :]
    return (_rms(x) * g) * (1.0 + sc) + sh


def _swiglu(h, wgu_ref, wd_ref):
    f = wd_ref.shape[0]
    acc = None
    for c0 in range(0, f, FFN_CHUNK):
        gate = jnp.dot(h, wgu_ref[:, c0:c0 + FFN_CHUNK], preferred_element_type=F32)
        up = jnp.dot(h, wgu_ref[:, f + c0:f + c0 + FFN_CHUNK], preferred_element_type=F32)
        a = (gate * jax.nn.sigmoid(gate) * up).astype(BF16)
        part = jnp.dot(a, wd_ref[c0:c0 + FFN_CHUNK, :], preferred_element_type=F32)
        acc = part if acc is None else acc + part
    return acc


def _gelu_tanh(x):
    return 0.5 * x * (1.0 + jnp.tanh(math.sqrt(2.0 / math.pi) * (x + 0.044715 * (x * x * x))))


def _ffn_kernel(x_ref, mods_ref, ng_ref, wgu_ref, wd_ref, o_ref):
    x = x_ref[0]
    h = _mod_norm(x, ng_ref[0:1, :], mods_ref, 0).astype(BF16)
    gt = mods_ref[0, 2:3, :]
    o_ref[0] = x + 0.5 * gt * _swiglu(h, wgu_ref, wd_ref)


def _ffn1(x, mods, mods_off, norm_g, wgu, wd, tm):
    b, t, d = x.shape
    return pl.pallas_call(
        _ffn_kernel,
        grid=(b, t // tm),
        in_specs=[pl.BlockSpec((1, tm, d), lambda bi, i: (bi, i, 0)),
                  pl.BlockSpec((1, N_MODS, d), lambda bi, i: (bi + mods_off, 0, 0)),
                  _const_spec(norm_g.shape), _const_spec(wgu.shape), _const_spec(wd.shape)],
        out_specs=pl.BlockSpec((1, tm, d), lambda bi, i: (bi, i, 0)),
        out_shape=jax.ShapeDtypeStruct((b, t, d), F32),
        compiler_params=_params(2),
        name="ffn1",
    )(x, mods, norm_g, wgu, wd)


def _proj_kernel(x_ref, mods_ref, ng_ref, win_ref, wgate_ref, bgate_ref, qg_ref, kg_ref, gmat_ref,
                 vng_ref, ws_ref, bs_ref, wbb_ref,
                 q_o, k_o, v_o, kb_o, vb_o, ga_o, gb_o, *maybe_gv_o, rows, q_scale):
    x = x_ref[0]
    tm, d = x.shape
    h = _mod_norm(x, ng_ref[1:2, :], mods_ref, 3).astype(BF16)

    def group_norm(z, g):
        ms = jnp.dot((z * z).astype(BF16), gmat_ref[...], preferred_element_type=F32)
        return z * lax.rsqrt(ms + EPS) * g

    def w_in(j):
        return win_ref[:, j * d:(j + 1) * d]

    q = group_norm(jnp.dot(h, w_in(0), preferred_element_type=F32), qg_ref[...])
    q_o[0] = (q * q_scale).astype(BF16)
    k = group_norm(jnp.dot(h, w_in(1), preferred_element_type=F32), kg_ref[...])
    k_o[0] = k
    kb_o[0] = k.astype(BF16)
    v = jnp.dot(h, w_in(2), preferred_element_type=F32)
    v_o[0] = v
    vb_o[0] = v.astype(BF16)

    u = _gelu_tanh(jnp.dot(h, w_in(3), preferred_element_type=F32))
    gz = _gelu_tanh(jnp.dot(h, w_in(4), preferred_element_type=F32))
    gv = _rms(gz) * vng_ref[...]
    if maybe_gv_o:
        maybe_gv_o[0][0] = gv
    gvb = gv.astype(BF16)

    row_i = lax.broadcasted_iota(jnp.int32, (rows, rows), 0)
    col_i = lax.broadcasted_iota(jnp.int32, (rows, rows), 1)
    tril = col_i <= row_i
    gw = d // GMLP_GROUPS
    ws = [jnp.where(tril, ws_ref[g], 0.0).astype(BF16) for g in range(GMLP_GROUPS)]
    s_chunks = []
    for c in range(tm // rows):
        r0 = c * rows
        mix = jnp.concatenate(
            [jnp.dot(ws[g], gvb[r0:r0 + rows, g * gw:(g + 1) * gw], preferred_element_type=F32)
             for g in range(GMLP_GROUPS)], axis=1)
        s_chunks.append(u[r0:r0 + rows, :] * (mix + bs_ref[...]))
    s = s_chunks[0] if len(s_chunks) == 1 else jnp.concatenate(s_chunks, axis=0)
    sb = jnp.dot(s.astype(BF16), wbb_ref[...], preferred_element_type=F32)

    gate_a = jax.nn.sigmoid(
        jnp.dot(h, wgate_ref[:, 0:d], preferred_element_type=F32) + bgate_ref[0:1, :])
    gate_b = jax.nn.sigmoid(
        jnp.dot(h, wgate_ref[:, d:2 * d], preferred_element_type=F32) + bgate_ref[1:2, :])
    ga_o[0] = gate_a.astype(BF16)
    gb_o[0] = (gate_b * sb).astype(BF16)


def _proj(x, mods, mods_off, norm_g, win, wgate, bgate, qg, kg, gmat, vng, ws, bs, wbb, tm, rows,
          q_scale, want_gv):
    b, t, d = x.shape
    row_spec = pl.BlockSpec((1, tm, d), lambda bi, i: (bi, i, 0))
    out_shapes = [jax.ShapeDtypeStruct((b, t, d), BF16),
                  jax.ShapeDtypeStruct((b, t, d), F32),
                  jax.ShapeDtypeStruct((b, t, d), F32),
                  jax.ShapeDtypeStruct((b, t, d), BF16),
                  jax.ShapeDtypeStruct((b, t, d), BF16),
                  jax.ShapeDtypeStruct((b, t, d), BF16),
                  jax.ShapeDtypeStruct((b, t, d), BF16)]
    if want_gv:
        out_shapes.append(jax.ShapeDtypeStruct((b, t, d), F32))
    consts = (norm_g, win, wgate, bgate, qg, kg, gmat, vng, ws, bs, wbb)
    return pl.pallas_call(
        functools.partial(_proj_kernel, rows=rows, q_scale=q_scale),
        grid=(b, t // tm),
        in_specs=[row_spec, pl.BlockSpec((1, N_MODS, d), lambda bi, i: (bi + mods_off, 0, 0))]
        + [_const_spec(a.shape) for a in consts],
        out_specs=[row_spec] * len(out_shapes),
        out_shape=out_shapes,
        compiler_params=_params(2),
        name="mixer_proj",
    )(x, mods, *consts)


def _split_maps(q):
    lane = lax.broadcasted_iota(jnp.int32, q.shape, 1)
    half = q.shape[1] // 2
    zero = jnp.zeros_like(q)
    return jnp.where(lane < half, q, zero), jnp.where(lane >= half, q, zero)


def _qk(a, k):
    return lax.dot_general(a, k, (((1,), (1,)), ((), ())), preferred_element_type=F32)


def _lambda(lq_ref, lam_init):
    lq = lq_ref[...]
    t1 = jnp.sum(lq[0:1, :] * lq[1:2, :], axis=-1, keepdims=True)
    t2 = jnp.sum(lq[2:3, :] * lq[3:4, :], axis=-1, keepdims=True)
    return jnp.exp(t1) - jnp.exp(t2) + lam_init


def _finish_heads(acc1, l1, acc2, l2, lam, sg, lam_init):
    o = acc1 / l1 - lam * (acc2 / l2)
    return (_rms(o) * sg) * (1.0 - lam_init)


def _attn_kernel(slope_ref, tile_ref, period_ref, bias_ref, first_ref, last_ref, dist_ref,
                 q_ref, k_ref, v_ref, eqt_ref, ek_ref, lq_ref, sg_ref, o_ref,
                 kx_scr, vt_scr, bias_scr, at_scr, s_scr, p_last_scr, smax_scr, m_scr, fin_scr,
                 acc_scr, *, lam_init, wide, tq, n_steps):
    h = pl.program_id(1)
    t_all = k_ref.shape[1]
    slope = slope_ref[h]
    per_wide = wide // tq

    kx_scr[:, 0:LANES] = k_ref[0]
    for c in range(t_all // wide):
        kx_scr[c * wide:(c + 1) * wide, LANES:2 * LANES] = ek_ref[0]
        vt_scr[0:LANES, c * wide:(c + 1) * wide] = (
            v_ref[0, c * wide:(c + 1) * wide, :].astype(F32).T.astype(BF16))
    vt_scr[LANES:, :] = jnp.ones((vt_scr.shape[0] - LANES, t_all), BF16)
    s_pos = lax.broadcasted_iota(jnp.int32, (wide, tq), 0)
    for c in range(per_wide):
        t_pos = c * tq + lax.broadcasted_iota(jnp.int32, (wide, tq), 1)
        visible = (s_pos // CHUNK) <= (t_pos // CHUNK)
        fix = (jnp.abs(t_pos - s_pos) - (t_pos - s_pos)).astype(F32)
        bias_scr[c] = jnp.where(visible, -slope * fix, NEG_BIG)
    p_last_scr[...] = jnp.zeros(p_last_scr.shape, BF16)
    acc_scr[...] = jnp.zeros(acc_scr.shape, F32)
    m_scr[...] = jnp.full(m_scr.shape, NEG_BIG, F32)

    def build_queries(tile):
        r0 = pl.multiple_of(tile * tq, tq)
        qt = q_ref[0, pl.ds(r0, tq), :].astype(F32).T
        row = lax.broadcasted_iota(jnp.int32, qt.shape, 0)
        zero = jnp.zeros_like(qt)
        buf = tile % 2
        at_scr[buf, 0, 0:LANES, :] = jnp.where(row < LANES // 2, qt, zero).astype(BF16)
        at_scr[buf, 1, 0:LANES, :] = jnp.where(row >= LANES // 2, qt, zero).astype(BF16)
        alibi_rows = eqt_ref[0, tile % per_wide]
        at_scr[buf, 0, LANES:2 * LANES, :] = alibi_rows
        at_scr[buf, 1, LANES:2 * LANES, :] = alibi_rows

    sub = MXU_DIM
    n_sub = wide // sub

    def scores(e, slot):
        kx = kx_scr[pl.ds(pl.multiple_of(period_ref[e] * wide, wide), wide), :]
        buf = tile_ref[e] % 2
        for mp in range(2):
            s = jnp.dot(kx, at_scr[buf, mp], preferred_element_type=F32)
            s_scr[slot, mp] = s
            smax_scr[slot, mp, 0:1, :] = jnp.max(s, axis=0, keepdims=True)

    def values_block(e, c):
        off = pl.multiple_of(period_ref[e] * wide + c * sub, sub)
        return vt_scr[:, pl.ds(off, sub)]

    def pipeline_step(e, slot, masked, p_prev):
        shift = -slope * (dist_ref[e] * wide).astype(F32)
        is_first = first_ref[e] == 1

        def block(mp, rows):
            s = s_scr[slot, mp, rows, :]
            return s + bias_scr[bias_ref[e], rows, :] if masked else s

        offsets, alphas = [], []
        for mp in range(2):
            if masked:
                s_max = None
                for c in range(n_sub):
                    blk_max = jnp.max(block(mp, slice(c * sub, (c + 1) * sub)), axis=0, keepdims=True)
                    s_max = blk_max if s_max is None else jnp.maximum(s_max, blk_max)
            else:
                s_max = smax_scr[slot, mp, 0:1, :]
            m_old = jnp.where(is_first, NEG_BIG, m_scr[mp, 0:1, :])
            m_new = jnp.maximum(m_old, s_max + shift)
            alphas.append(jnp.exp2(m_old - m_new))
            m_scr[mp, 0:1, :] = m_new
            offsets.append(m_new - shift)
        vt_pending = values_block(e - 1, n_sub - 1)
        pending = [jnp.dot(vt_pending, p_prev[mp], preferred_element_type=F32) for mp in range(2)]
        p_last = [None, None]
        scores(e + 1, 1 - slot)
        pv = [None, None]
        for c in range(n_sub):
            rows = slice(c * sub, (c + 1) * sub)
            for mp in range(2):
                p = jnp.exp2(block(mp, rows) - offsets[mp]).astype(BF16)
                if c == n_sub - 1:
                    p_last[mp] = p
                else:
                    part = jnp.dot(values_block(e, c), p, preferred_element_type=F32)
                    pv[mp] = part if pv[mp] is None else pv[mp] + part
        for mp in range(2):
            done = acc_scr[mp] + pending[mp]
            fin_scr[slot, mp] = done
            acc_scr[mp] = alphas[mp] * done + pv[mp]
        return p_last

    def finalize(tile, slot):
        lam = _lambda(lq_ref, lam_init)
        a1 = fin_scr[slot, 0]
        a2 = fin_scr[slot, 1]
        ot = a1[0:LANES] / a1[LANES:LANES + 1] - lam * (a2[0:LANES] / a2[LANES:LANES + 1])
        ot = ot * lax.rsqrt(jnp.mean(ot * ot, axis=0, keepdims=True) + EPS)
        r0 = pl.multiple_of(tile * tq, tq)
        o_ref[0, pl.ds(r0, tq), :] = ((ot.T * sg_ref[...]) * (1.0 - lam_init)).astype(o_ref.dtype)

    build_queries(tile_ref[1])
    scores(1, 0)

    def iteration(kk, carry):
        e = 2 * kk + 1

        for nxt in (e + 1, e + 2):
            @pl.when(first_ref[nxt] == 1)
            def _():
                build_queries(tile_ref[nxt])

        masked = [bias_ref[e] < per_wide, bias_ref[e + 1] < per_wide]
        for pattern in ((False, False), (False, True), (True, False), (True, True)):
            conds = [m if want else jnp.logical_not(m) for m, want in zip(masked, pattern)]

            @pl.when(jnp.logical_and(conds[0], conds[1]))
            def _():
                p_mid = pipeline_step(e, 0, pattern[0], [p_last_scr[0], p_last_scr[1]])
                p_end = pipeline_step(e + 1, 1, pattern[1], p_mid)
                p_last_scr[0] = p_end[0]
                p_last_scr[1] = p_end[1]

        for slot in (0, 1):
            @pl.when(last_ref[e - 1 + slot] == 1)
            def _():
                finalize(tile_ref[e - 1 + slot], slot)

        return carry

    lax.fori_loop(0, n_steps // 2 + 1, iteration, 0)


def _bf16_pieces(x, n):
    pieces, rest = [], np.asarray(x, np.float32)
    for _ in range(n):
        p = rest.astype(BF16).astype(np.float32)
        pieces.append(p)
        rest = rest - p
    return pieces


def _alibi_rows_cols(slopes, period):
    n_heads = slopes.shape[0]
    pos = np.arange(period)
    offs = ((pos // ALIBI_SPLIT) * ALIBI_SPLIT, pos % ALIBI_SPLIT)
    eqt = np.zeros((n_heads, LANES, period), np.float32)
    ek = np.zeros((n_heads, period, LANES), np.float32)
    lane = 0
    for piece in _bf16_pieces(slopes, 3):
        for off in offs:
            eqt[:, lane, :] = off[None, :]
            ek[:, :, lane] = -piece[:, None]
            eqt[:, lane + 1, :] = piece[:, None]
            ek[:, :, lane + 1] = off[None, :]
            lane += 2
    return jnp.asarray(eqt, BF16), jnp.asarray(ek, BF16)


def _attention_schedule(n_tiles, per_wide):
    tile, period, bias, first, last, dist = [], [], [], [], [], []
    for i in range(n_tiles):
        n_wide = i // per_wide
        for j in range(n_wide + 1):
            own = j == n_wide
            tile.append(i)
            period.append(j)
            bias.append(i % per_wide if own else per_wide)
            first.append(int(j == 0))
            last.append(int(own))
            dist.append(n_wide - j)
    pad = lambda v, fill: np.asarray([fill] + v + [fill] * 3, np.int32)
    return (pad(tile, n_tiles - 1), pad(period, 0), pad(bias, per_wide), pad(first, 1),
            pad(last, 0), pad(dist, 0)), len(tile)


def _attention(q, kb, vb, slopes, lq, sg, lam_init):
    b, t, width = q.shape
    n_heads = width // LANES
    tq = min(ATTN_TQ, t)
    wide = min(ALIBI_PERIOD, t)
    assert t % wide == 0 and wide % tq == 0
    per_wide = wide // tq
    eqt, ek = _alibi_rows_cols(slopes, wide)
    eqt = eqt.reshape(n_heads, LANES, per_wide, tq).transpose(0, 2, 1, 3)
    tables, n_steps = _attention_schedule(t // tq, per_wide)
    ones_rows = 16
    n_prefetch = 1 + len(tables)
    head_spec = pl.BlockSpec((1, t, LANES), lambda bi, h, *_: (bi, 0, h))
    grid_spec = pltpu.PrefetchScalarGridSpec(
        num_scalar_prefetch=n_prefetch,
        grid=(b, n_heads),
        in_specs=[head_spec, head_spec, head_spec,
                  pl.BlockSpec((1, per_wide, LANES, tq), lambda bi, h, *_: (h, 0, 0, 0)),
                  pl.BlockSpec((1, wide, LANES), lambda bi, h, *_: (h, 0, 0)),
                  pl.BlockSpec(lq.shape, lambda bi, h, *_: (0, 0)),
                  pl.BlockSpec(sg.shape, lambda bi, h, *_: (0, 0))],
        out_specs=head_spec,
        scratch_shapes=[pltpu.VMEM((t, 2 * LANES), BF16),
                        pltpu.VMEM((LANES + ones_rows, t), BF16),
                        pltpu.VMEM((per_wide, wide, tq), F32),
                        pltpu.VMEM((2, 2, 2 * LANES, tq), BF16),
                        pltpu.VMEM((2, 2, wide, tq), F32),
                        pltpu.VMEM((2, MXU_DIM, tq), BF16),
                        pltpu.VMEM((2, 2, 8, tq), F32),
                        pltpu.VMEM((2, 8, tq), F32),
                        pltpu.VMEM((2, 2, LANES + ones_rows, tq), F32),
                        pltpu.VMEM((2, LANES + ones_rows, tq), F32)],
    )
    return pl.pallas_call(
        functools.partial(_attn_kernel, lam_init=lam_init, wide=wide, tq=tq, n_steps=n_steps),
        grid_spec=grid_spec,
        out_shape=jax.ShapeDtypeStruct((b, t, width), BF16),
        compiler_params=_params(2),
        name="diff_attention",
    )(jnp.asarray(slopes, F32), *tables, q, kb, vb, eqt, ek, lq, sg)


def _attn_cache_kernel(slope_ref, q_ref, kn_ref, vn_ref, ck_ref, cv_ref, lq_ref, sg_ref, o_ref, *,
                       lam_init):
    h = pl.program_id(1)
    slope = slope_ref[h]
    tn = q_ref.shape[1]
    past = ck_ref.shape[1]
    q1, q2 = _split_maps(q_ref[0])
    ck = ck_ref[0].astype(BF16)
    cv = cv_ref[0].astype(BF16)
    kn = kn_ref[0]
    vn = vn_ref[0]

    def bias_for(k0, width):
        qpos = past + lax.broadcasted_iota(jnp.int32, (tn, width), 0)
        kpos = k0 + lax.broadcasted_iota(jnp.int32, (tn, width), 1)
        visible = (kpos // CHUNK) <= (qpos // CHUNK)
        return jnp.where(visible, -slope * jnp.abs(qpos - kpos).astype(F32), NEG_BIG)

    bias_c = bias_for(0, past)
    bias_n = bias_for(past, tn)

    def one_map(a):
        sc = _qk(a, ck) + bias_c
        sn = _qk(a, kn) + bias_n
        m = jnp.maximum(jnp.max(sc, axis=-1, keepdims=True), jnp.max(sn, axis=-1, keepdims=True))
        pc = jnp.exp2(sc - m)
        pn = jnp.exp2(sn - m)
        l = jnp.sum(pc, axis=-1, keepdims=True) + jnp.sum(pn, axis=-1, keepdims=True)
        acc = (jnp.dot(pc.astype(BF16), cv, preferred_element_type=F32)
               + jnp.dot(pn.astype(BF16), vn, preferred_element_type=F32))
        return l, acc

    l1, acc1 = one_map(q1)
    l2, acc2 = one_map(q2)
    lam = _lambda(lq_ref, lam_init)
    o_ref[0] = _finish_heads(acc1, l1, acc2, l2, lam, sg_ref[...], lam_init).astype(o_ref.dtype)


def _attention_cached(q, kb, vb, cache_k, cache_v, slopes, lq, sg, lam_init):
    b, tn, width = q.shape
    past = cache_k.shape[1]
    n_heads = width // LANES
    new_spec = pl.BlockSpec((1, tn, LANES), lambda bi, h, s: (bi, 0, h))
    past_spec = pl.BlockSpec((1, past, LANES), lambda bi, h, s: (bi, 0, h))
    grid_spec = pltpu.PrefetchScalarGridSpec(
        num_scalar_prefetch=1,
        grid=(b, n_heads),
        in_specs=[new_spec, new_spec, new_spec, past_spec, past_spec,
                  pl.BlockSpec(lq.shape, lambda bi, h, s: (0, 0)),
                  pl.BlockSpec(sg.shape, lambda bi, h, s: (0, 0))],
        out_specs=new_spec,
    )
    return pl.pallas_call(
        functools.partial(_attn_cache_kernel, lam_init=lam_init),
        grid_spec=grid_spec,
        out_shape=jax.ShapeDtypeStruct((b, tn, width), BF16),
        compiler_params=_params(2),
        name="diff_attention_cached",
    )(jnp.asarray(slopes, F32), q, kb, vb, cache_k, cache_v, lq, sg)


def _out_kernel(x_ref, o_ref, ga_ref, gb_ref, mods_ref, ng_ref, wba_ref, wout_ref, wgu_ref, wd_ref,
                y_ref):
    x = x_ref[0]
    oa = jnp.dot(o_ref[0], wba_ref[...], preferred_element_type=F32)
    mixed = (ga_ref[0].astype(F32) * oa + gb_ref[0].astype(F32)).astype(BF16)
    x = x + mods_ref[0, 5:6, :] * jnp.dot(mixed, wout_ref[...], preferred_element_type=F32)
    h = _mod_norm(x, ng_ref[2:3, :], mods_ref, 6).astype(BF16)
    x = x + 0.5 * mods_ref[0, 8:9, :] * _swiglu(h, wgu_ref, wd_ref)
    y_ref[0] = _rms(x) * ng_ref[3:4, :]


def _mix_ffn2(x, o, ga, gb, mods, mods_off, norm_g, wba, wout, wgu, wd, tm):
    b, t, d = x.shape
    row_spec = pl.BlockSpec((1, tm, d), lambda bi, i: (bi, i, 0))
    consts = (norm_g, wba, wout, wgu, wd)
    return pl.pallas_call(
        _out_kernel,
        grid=(b, t // tm),
        in_specs=[row_spec] * 4 + [pl.BlockSpec((1, N_MODS, d), lambda bi, i: (bi + mods_off, 0, 0))]
        + [_const_spec(a.shape) for a in consts],
        out_specs=row_spec,
        out_shape=jax.ShapeDtypeStruct((b, t, d), F32),
        compiler_params=_params(2),
        name="mix_ffn2",
    )(x, o, ga, gb, mods, *consts)


def _row_tile(t, target):
    tm = min(t, target)
    while t % tm:
        tm //= 2
    return tm


def kernel(x_prompt, x_sample, cache_k, cache_v, c_prompt, c_sample, ada_w, ada_b, norm_g, ffn1_wgu,
           ffn1_wd, w_in, q_norm_g, k_norm_g, lambda_qk, attn_subln_g, gmlp_vnorm_g, gmlp_ws, gmlp_bs,
           w_gate, b_gate, w_branch, w_out, ffn2_wgu, ffn2_wd):
    assert ada_w.shape[0] == 1, "single-layer step"
    bp, tp, d = x_prompt.shape
    bs_, ts, _ = x_sample.shape
    _, _, past, n_heads, _, hd = cache_k.shape
    assert n_heads == 8 and 2 * hd == LANES and n_heads * LANES == d
    lam_init = _lambda_init(0)
    log2e = math.log2(math.e)
    slopes = (2.0 ** (-8.0 * np.arange(1, n_heads + 1) / n_heads) * log2e).astype(np.float32)
    q_scale = float(hd) ** -0.5 * log2e

    assert ffn1_wd.shape[1] % FFN_CHUNK == 0 and w_in.shape[2] == 5 * d
    wgu1, wd1 = ffn1_wgu[0].astype(BF16), ffn1_wd[0].astype(BF16)
    wgu2, wd2 = ffn2_wgu[0].astype(BF16), ffn2_wd[0].astype(BF16)
    win = w_in[0].astype(BF16)
    wgate = w_gate[0].astype(BF16)
    bgate = b_gate[0].reshape(2, d)
    wba = w_branch[0, :d].astype(BF16)
    wbb = w_branch[0, d:].astype(BF16)
    wout = w_out[0].astype(BF16)
    ng = norm_g[0]
    qg = jnp.tile(q_norm_g[0], d // hd).reshape(1, d)
    kg = jnp.tile(k_norm_g[0], d // hd).reshape(1, d)
    grp = np.arange(d) // hd
    gmat = jnp.asarray((grp[:, None] == grp[None, :]).astype(np.float32) / hd, BF16)
    vng = gmlp_vnorm_g[0].reshape(1, d)
    sg = attn_subln_g[0].reshape(1, LANES)
    lq = lambda_qk[0]

    mods = _mods(jnp.concatenate([c_prompt, c_sample], axis=0), ada_w[0], ada_b[0])
    mods = mods.reshape(bp + bs_, N_MODS, d)

    def gmlp_consts(t):
        rows = min(t, GMLP_CHUNK)
        ws = gmlp_ws[0][:, :rows, :rows]
        bs = jnp.repeat(gmlp_bs[0][:, :rows].T, d // GMLP_GROUPS, axis=1)
        return rows, ws, bs

    def layer(x, mods_off, attend, tm_ffn, tm_proj, want_gv):
        b, t, _ = x.shape
        rows, ws, bs = gmlp_consts(t)
        x1 = _ffn1(x, mods, mods_off, ng, wgu1, wd1, _row_tile(t, tm_ffn))
        tmp = max(_row_tile(t, tm_proj), rows)
        outs = _proj(x1, mods, mods_off, ng, win, wgate, bgate, qg, kg, gmat, vng, ws, bs, wbb,
                     tmp, rows, q_scale, want_gv)
        q, k, v, kb, vb, ga, gb = outs[:7]
        o = attend(q, kb, vb)
        y = _mix_ffn2(x1, o, ga, gb, mods, mods_off, ng, wba, wout, wgu2, wd2,
                      _row_tile(t, tm_ffn))
        return y, k, v, (outs[7] if want_gv else None)

    y_p, k_p, v_p, _ = layer(
        x_prompt, 0, lambda q, kb, vb: _attention(q, kb, vb, slopes, lq, sg, lam_init),
        512, 512, False)

    ck = cache_k[0].reshape(bs_, past, d)
    cv = cache_v[0].reshape(bs_, past, d)
    y_s, k_s, v_s, gv_s = layer(
        x_sample, bp,
        lambda q, kb, vb: _attention_cached(q, kb, vb, ck, cv, slopes, lq, sg, lam_init),
        512, 256, True)

    return (y_p, y_s,
            k_p.reshape(1, bp, tp, n_heads, 2, hd), v_p.reshape(1, bp, tp, n_heads, 2 * hd),
            k_s.reshape(1, bs_, ts, n_heads, 2, hd), v_s.reshape(1, bs_, ts, n_heads, 2 * hd),
            gv_s.reshape(1, bs_, ts, d))
```

```python
import functools
import math

import numpy as np
import jax
import jax.numpy as jnp
from jax import lax
from jax.experimental import pallas as pl
from jax.experimental.pallas import tpu as pltpu

F32 = jnp.float32
BF16 = jnp.bfloat16

EPS = 1e-6
CHUNK = 64
GMLP_CHUNK = 128
GMLP_GROUPS = 8
N_MODS = 9
NEG_BIG = -1e30

V7X_VMEM_BYTES = 64 * 1024 * 1024
VMEM_LIMIT_BYTES = V7X_VMEM_BYTES - 8 * 1024 * 1024
LANES = 128
MXU_DIM = 256
ATTN_TQ = 512
FFN_CHUNK = 256
ALIBI_SPLIT = 256
ALIBI_PERIOD = 1024


def _lambda_init(layer_idx):
    return 0.8 - 0.6 * math.exp(-0.3 * layer_idx)


def _const_spec(shape):
    nd = len(shape)
    return pl.BlockSpec(shape, lambda *_: (0,) * nd, pipeline_mode=pl.Buffered(1))


def _params(n_axes):
    return pltpu.CompilerParams(
        dimension_semantics=("arbitrary",) * n_axes, vmem_limit_bytes=VMEM_LIMIT_BYTES)


def _mods_kernel(c_ref, w_ref, b_ref, o_ref):
    c = c_ref[...]
    h = c * jax.nn.sigmoid(c)
    o_ref[...] = jnp.dot(h, w_ref[...], preferred_element_type=F32,
                         precision=lax.Precision.HIGHEST) + b_ref[...]


def _mods(c_all, ada_w, ada_b):
    n, d = c_all.shape
    width = ada_w.shape[1]
    tn = d
    return pl.pallas_call(
        _mods_kernel,
        grid=(width // tn,),
        in_specs=[pl.BlockSpec((n, d), lambda j: (0, 0)),
                  pl.BlockSpec((d, tn), lambda j: (0, j)),
                  pl.BlockSpec((1, tn), lambda j: (0, j))],
        out_specs=pl.BlockSpec((n, tn), lambda j: (0, j)),
        out_shape=jax.ShapeDtypeStruct((n, width), F32),
        compiler_params=_params(1),
        name="adaln_mods",
    )(c_all, ada_w, ada_b.reshape(1, width))


def _rms(x):
    return x * lax.rsqrt(jnp.mean(x * x, axis=-1, keepdims=True) + EPS)


def _mod_norm(x, g, mods_ref, first):
    return (_rms(x) * g) * (1.0 + mods_ref[0, first + 1]) + mods_ref[0, first]


def _swiglu(h, wgu_ref, wd_ref):
    f = wd_ref.shape[0]
    acc = None
    for c0 in range(0, f, FFN_CHUNK):
        gate = jnp.dot(h, wgu_ref[:, c0:c0 + FFN_CHUNK], preferred_element_type=F32)
        up = jnp.dot(h, wgu_ref[:, f + c0:f + c0 + FFN_CHUNK], preferred_element_type=F32)
        a = (gate * jax.nn.sigmoid(gate) * up).astype(BF16)
        part = jnp.dot(a, wd_ref[c0:c0 + FFN_CHUNK, :], preferred_element_type=F32)
        acc = part if acc is None else acc + part
    return acc


def _gelu_tanh(x):
    return 0.5 * x * (1.0 + jnp.tanh(math.sqrt(2.0 / math.pi) * (x + 0.044715 * (x * x * x))))


def _row_specs(x, mods, tm):
    _, _, d = x.shape
    r = mods.shape[2]
    row_spec = pl.BlockSpec((1, tm, d), lambda g, i: (g, i, 0))
    if r == 1:
        mods_spec = pl.BlockSpec((1, N_MODS, 1, d), lambda g, i: (g, 0, 0, 0))
    else:
        mods_spec = pl.BlockSpec((1, N_MODS, tm, d), lambda g, i: (g, 0, i, 0))
    return row_spec, mods_spec


def _ffn_kernel(x_ref, mods_ref, ng_ref, wgu_ref, wd_ref, o_ref):
    x = x_ref[0]
    h = _mod_norm(x, ng_ref[0:1, :], mods_ref, 0).astype(BF16)
    o_ref[0] = x + 0.5 * mods_ref[0, 2] * _swiglu(h, wgu_ref, wd_ref)


def _ffn1(x, mods, norm_g, wgu, wd, tm):
    g, t, d = x.shape
    row_spec, mods_spec = _row_specs(x, mods, tm)
    return pl.pallas_call(
        _ffn_kernel,
        grid=(g, t // tm),
        in_specs=[row_spec, mods_spec,
                  _const_spec(norm_g.shape), _const_spec(wgu.shape), _const_spec(wd.shape)],
        out_specs=row_spec,
        out_shape=jax.ShapeDtypeStruct((g, t, d), F32),
        compiler_params=_params(2),
        name="ffn1",
    )(x, mods, norm_g, wgu, wd)


def _proj_kernel(x_ref, mods_ref, ng_ref, win_ref, wgate_ref, bgate_ref, qg_ref, kg_ref, gsum_ref,
                 gbcast_ref, vng_ref, ws_ref, bs_ref, wbb_ref,
                 q_o, k_o, v_o, kb_o, vb_o, ga_o, gb_o, *maybe_gv_o, rows, q_scale):
    x = x_ref[0]
    tm, d = x.shape
    h = _mod_norm(x, ng_ref[1:2, :], mods_ref, 3).astype(BF16)

    def w_in(j):
        return win_ref[:, j * d:(j + 1) * d]

    def group_norm(z, g):
        ms = jnp.dot((z * z).astype(BF16), gsum_ref[...], preferred_element_type=F32)
        r = lax.rsqrt(ms + EPS)
        r_hi = r.astype(BF16)
        r_lo = (r - r_hi.astype(F32)).astype(BF16)
        scale = jnp.dot(jnp.concatenate([r_hi, r_lo], axis=1), gbcast_ref[...],
                        preferred_element_type=F32)
        return z * scale * g

    q = group_norm(jnp.dot(h, w_in(0), preferred_element_type=F32), qg_ref[...])
    q_o[0] = (q * q_scale).astype(BF16)
    k = group_norm(jnp.dot(h, w_in(1), preferred_element_type=F32), kg_ref[...])
    k_o[0] = k
    kb_o[0] = k.astype(BF16)
    v = jnp.dot(h, w_in(2), preferred_element_type=F32)
    v_o[0] = v
    vb_o[0] = v.astype(BF16)

    u = _gelu_tanh(jnp.dot(h, w_in(3), preferred_element_type=F32))
    gz = _gelu_tanh(jnp.dot(h, w_in(4), preferred_element_type=F32))
    gv = _rms(gz) * vng_ref[...]
    if maybe_gv_o:
        maybe_gv_o[0][0] = gv
    gvb = gv.astype(BF16)

    row_i = lax.broadcasted_iota(jnp.int32, (rows, rows), 0)
    col_i = lax.broadcasted_iota(jnp.int32, (rows, rows), 1)
    tril = col_i <= row_i
    gw = d // GMLP_GROUPS
    ws = [jnp.where(tril, ws_ref[g], 0.0).astype(BF16) for g in range(GMLP_GROUPS)]
    s_chunks = []
    for c in range(tm // rows):
        r0 = c * rows
        mix = jnp.concatenate(
            [jnp.dot(ws[g], gvb[r0:r0 + rows, g * gw:(g + 1) * gw], preferred_element_type=F32)
             for g in range(GMLP_GROUPS)], axis=1)
        s_chunks.append(u[r0:r0 + rows, :] * (mix + bs_ref[...]))
    s = s_chunks[0] if len(s_chunks) == 1 else jnp.concatenate(s_chunks, axis=0)
    sb = jnp.dot(s.astype(BF16), wbb_ref[...], preferred_element_type=F32)

    gate_a = jax.nn.sigmoid(
        jnp.dot(h, wgate_ref[:, 0:d], preferred_element_type=F32) + bgate_ref[0:1, :])
    gate_b = jax.nn.sigmoid(
        jnp.dot(h, wgate_ref[:, d:2 * d], preferred_element_type=F32) + bgate_ref[1:2, :])
    ga_o[0] = gate_a.astype(BF16)
    gb_o[0] = (gate_b * sb).astype(BF16)


def _proj(x, mods, norm_g, win, wgate, bgate, qg, kg, gsum, gbcast, vng, ws, bs, wbb, tm, rows,
          q_scale, want_gv):
    g, t, d = x.shape
    row_spec, mods_spec = _row_specs(x, mods, tm)
    out_shapes = [jax.ShapeDtypeStruct((g, t, d), BF16),
                  jax.ShapeDtypeStruct((g, t, d), F32),
                  jax.ShapeDtypeStruct((g, t, d), F32),
                  jax.ShapeDtypeStruct((g, t, d), BF16),
                  jax.ShapeDtypeStruct((g, t, d), BF16),
                  jax.ShapeDtypeStruct((g, t, d), BF16),
                  jax.ShapeDtypeStruct((g, t, d), BF16)]
    if want_gv:
        out_shapes.append(jax.ShapeDtypeStruct((g, t, d), F32))
    consts = (norm_g, win, wgate, bgate, qg, kg, gsum, gbcast, vng, ws, bs, wbb)
    return pl.pallas_call(
        functools.partial(_proj_kernel, rows=rows, q_scale=q_scale),
        grid=(g, t // tm),
        in_specs=[row_spec, mods_spec] + [_const_spec(a.shape) for a in consts],
        out_specs=[row_spec] * len(out_shapes),
        out_shape=out_shapes,
        compiler_params=_params(2),
        name="mixer_proj",
    )(x, mods, *consts)


def _split_maps(q):
    lane = lax.broadcasted_iota(jnp.int32, q.shape, 1)
    half = q.shape[1] // 2
    zero = jnp.zeros_like(q)
    return jnp.where(lane < half, q, zero), jnp.where(lane >= half, q, zero)


def _qk(a, k):
    return lax.dot_general(a, k, (((1,), (1,)), ((), ())), preferred_element_type=F32)


def _lambda(lq_ref, lam_init):
    lq = lq_ref[...]
    t1 = jnp.sum(lq[0:1, :] * lq[1:2, :], axis=-1, keepdims=True)
    t2 = jnp.sum(lq[2:3, :] * lq[3:4, :], axis=-1, keepdims=True)
    return jnp.exp(t1) - jnp.exp(t2) + lam_init


def _finish_heads(acc1, l1, acc2, l2, lam, sg, lam_init):
    o = acc1 / l1 - lam * (acc2 / l2)
    return (_rms(o) * sg) * (1.0 - lam_init)


def _attn_kernel(slope_ref, tile_ref, period_ref, bias_ref, first_ref, last_ref, dist_ref,
                 q_ref, k_ref, v_ref, eqt_ref, ek_ref, lq_ref, sg_ref, o_ref,
                 kx_scr, vt_scr, bias_scr, at_scr, s_scr, p_last_scr, smax_scr, m_scr, fin_scr,
                 acc_scr, *, lam_init, wide, tq, n_steps):
    h = pl.program_id(1)
    t_all = k_ref.shape[1]
    slope = slope_ref[h]
    per_wide = wide // tq

    kx_scr[:, 0:LANES] = k_ref[0]
    for c in range(t_all // wide):
        kx_scr[c * wide:(c + 1) * wide, LANES:2 * LANES] = ek_ref[0]
        vt_scr[0:LANES, c * wide:(c + 1) * wide] = (
            v_ref[0, c * wide:(c + 1) * wide, :].astype(F32).T.astype(BF16))
    vt_scr[LANES:, :] = jnp.ones((vt_scr.shape[0] - LANES, t_all), BF16)
    s_pos = lax.broadcasted_iota(jnp.int32, (wide, tq), 0)
    for c in range(per_wide):
        t_pos = c * tq + lax.broadcasted_iota(jnp.int32, (wide, tq), 1)
        visible = (s_pos // CHUNK) <= (t_pos // CHUNK)
        fix = (jnp.abs(t_pos - s_pos) - (t_pos - s_pos)).astype(F32)
        bias_scr[c] = jnp.where(visible, -slope * fix, NEG_BIG)
    p_last_scr[...] = jnp.zeros(p_last_scr.shape, BF16)
    acc_scr[...] = jnp.zeros(acc_scr.shape, F32)
    m_scr[...] = jnp.full(m_scr.shape, NEG_BIG, F32)

    def build_queries(tile):
        r0 = pl.multiple_of(tile * tq, tq)
        qt = q_ref[0, pl.ds(r0, tq), :].astype(F32).T
        row = lax.broadcasted_iota(jnp.int32, qt.shape, 0)
        zero = jnp.zeros_like(qt)
        at_scr[0, 0:LANES, :] = jnp.where(row < LANES // 2, qt, zero).astype(BF16)
        at_scr[1, 0:LANES, :] = jnp.where(row >= LANES // 2, qt, zero).astype(BF16)
        alibi_rows = eqt_ref[0, tile % per_wide]
        at_scr[0, LANES:2 * LANES, :] = alibi_rows
        at_scr[1, LANES:2 * LANES, :] = alibi_rows

    sub = MXU_DIM
    n_sub = wide // sub

    def scores(e, slot, masked):
        kx = kx_scr[pl.ds(pl.multiple_of(period_ref[e] * wide, wide), wide), :]
        for mp in range(2):
            s = jnp.dot(kx, at_scr[mp], preferred_element_type=F32)
            if masked:
                s = s + bias_scr[bias_ref[e]]
            s_scr[slot, mp] = s
            smax_scr[slot, mp, 0:1, :] = jnp.max(s, axis=0, keepdims=True)

    def values_block(e, c):
        off = pl.multiple_of(period_ref[e] * wide + c * sub, sub)
        return vt_scr[:, pl.ds(off, sub)]

    def pipeline_step(e, slot, masked):
        shift = -slope * (dist_ref[e] * wide).astype(F32)
        is_first = first_ref[e] == 1
        offsets, alphas = [], []
        for mp in range(2):
            m_old = jnp.where(is_first, NEG_BIG, m_scr[mp, 0:1, :])
            m_new = jnp.maximum(m_old, smax_scr[slot, mp, 0:1, :] + shift)
            alphas.append(jnp.exp2(m_old - m_new))
            m_scr[mp, 0:1, :] = m_new
            offsets.append(m_new - shift)
        vt_pending = values_block(e - 1, n_sub - 1)
        pending = [jnp.dot(vt_pending, p_last_scr[mp], preferred_element_type=F32) for mp in range(2)]
        scores(e + 1, 1 - slot, masked)
        pv = [None, None]
        for c in range(n_sub):
            rows = slice(c * sub, (c + 1) * sub)
            for mp in range(2):
                p = jnp.exp2(s_scr[slot, mp, rows, :] - offsets[mp]).astype(BF16)
                if c == n_sub - 1:
                    p_last_scr[mp] = p
                else:
                    part = jnp.dot(values_block(e, c), p, preferred_element_type=F32)
                    pv[mp] = part if pv[mp] is None else pv[mp] + part
        for mp in range(2):
            done = acc_scr[mp] + pending[mp]
            fin_scr[mp] = done
            acc_scr[mp] = alphas[mp] * done + pv[mp]

    def finalize(tile):
        lam = _lambda(lq_ref, lam_init)
        a1 = fin_scr[0]
        a2 = fin_scr[1]
        ot = a1[0:LANES] / a1[LANES:LANES + 1] - lam * (a2[0:LANES] / a2[LANES:LANES + 1])
        ot = ot * lax.rsqrt(jnp.mean(ot * ot, axis=0, keepdims=True) + EPS)
        r0 = pl.multiple_of(tile * tq, tq)
        o_ref[0, pl.ds(r0, tq), :] = ((ot.T * sg_ref[...]) * (1.0 - lam_init)).astype(o_ref.dtype)

    @pl.when(first_ref[1] == 1)
    def _():
        build_queries(tile_ref[1])

    @pl.when(first_ref[1] == 1)
    def _():
        scores(1, 0, True)

    def iteration(k, carry):
        e = k + 1

        @pl.when(first_ref[e + 1] == 1)
        def _():
            build_queries(tile_ref[e + 1])

        next_masked = bias_ref[e + 1] < per_wide
        for slot in (0, 1):
            for masked in (False, True):
                cond = next_masked if masked else jnp.logical_not(next_masked)

                @pl.when(jnp.logical_and(k % 2 == slot, cond))
                def _():
                    pipeline_step(e, slot, masked)

        @pl.when(last_ref[e - 1] == 1)
        def _():
            finalize(tile_ref[e - 1])

        return carry

    lax.fori_loop(0, n_steps + 1, iteration, 0)


def _bf16_pieces(x, n):
    pieces, rest = [], np.asarray(x, np.float32)
    for _ in range(n):
        p = rest.astype(BF16).astype(np.float32)
        pieces.append(p)
        rest = rest - p
    return pieces


def _alibi_rows_cols(slopes, period):
    n_heads = slopes.shape[0]
    pos = np.arange(period)
    offs = ((pos // ALIBI_SPLIT) * ALIBI_SPLIT, pos % ALIBI_SPLIT)
    eqt = np.zeros((n_heads, LANES, period), np.float32)
    ek = np.zeros((n_heads, period, LANES), np.float32)
    lane = 0
    for piece in _bf16_pieces(slopes, 3):
        for off in offs:
            eqt[:, lane, :] = off[None, :]
            ek[:, :, lane] = -piece[:, None]
            eqt[:, lane + 1, :] = piece[:, None]
            ek[:, :, lane + 1] = off[None, :]
            lane += 2
    return jnp.asarray(eqt, BF16), jnp.asarray(ek, BF16)


def _attention_schedule(n_tiles, per_wide):
    tile, period, bias, first, last, dist = [], [], [], [], [], []
    for i in range(n_tiles):
        n_wide = i // per_wide
        for j in range(n_wide + 1):
            own = j == n_wide
            tile.append(i)
            period.append(j)
            bias.append(i % per_wide if own else per_wide)
            first.append(int(j == 0))
            last.append(int(own))
            dist.append(n_wide - j)
    pad = lambda v, fill: np.asarray([fill] + v + [fill, fill], np.int32)
    return (pad(tile, n_tiles - 1), pad(period, 0), pad(bias, per_wide), pad(first, 1),
            pad(last, 0), pad(dist, 0)), len(tile)


def _attention(q, kb, vb, slopes, lq, sg, lam_init):
    b, t, width = q.shape
    n_heads = width // LANES
    tq = min(ATTN_TQ, t)
    wide = min(ALIBI_PERIOD, t)
    assert t % wide == 0 and wide % tq == 0
    per_wide = wide // tq
    eqt, ek = _alibi_rows_cols(slopes, wide)
    eqt = eqt.reshape(n_heads, LANES, per_wide, tq).transpose(0, 2, 1, 3)
    tables, n_steps = _attention_schedule(t // tq, per_wide)
    ones_rows = 16
    n_prefetch = 1 + len(tables)
    head_spec = pl.BlockSpec((1, t, LANES), lambda bi, h, *_: (bi, 0, h))
    grid_spec = pltpu.PrefetchScalarGridSpec(
        num_scalar_prefetch=n_prefetch,
        grid=(b, n_heads),
        in_specs=[head_spec, head_spec, head_spec,
                  pl.BlockSpec((1, per_wide, LANES, tq), lambda bi, h, *_: (h, 0, 0, 0)),
                  pl.BlockSpec((1, wide, LANES), lambda bi, h, *_: (h, 0, 0)),
                  pl.BlockSpec(lq.shape, lambda bi, h, *_: (0, 0)),
                  pl.BlockSpec(sg.shape, lambda bi, h, *_: (0, 0))],
        out_specs=head_spec,
        scratch_shapes=[pltpu.VMEM((t, 2 * LANES), BF16),
                        pltpu.VMEM((LANES + ones_rows, t), BF16),
                        pltpu.VMEM((per_wide, wide, tq), F32),
                        pltpu.VMEM((2, 2 * LANES, tq), BF16),
                        pltpu.VMEM((2, 2, wide, tq), F32),
                        pltpu.VMEM((2, MXU_DIM, tq), BF16),
                        pltpu.VMEM((2, 2, 8, tq), F32),
                        pltpu.VMEM((2, 8, tq), F32),
                        pltpu.VMEM((2, LANES + ones_rows, tq), F32),
                        pltpu.VMEM((2, LANES + ones_rows, tq), F32)],
    )
    return pl.pallas_call(
        functools.partial(_attn_kernel, lam_init=lam_init, wide=wide, tq=tq, n_steps=n_steps),
        grid_spec=grid_spec,
        out_shape=jax.ShapeDtypeStruct((b, t, width), BF16),
        compiler_params=_params(2),
        name="diff_attention",
    )(jnp.asarray(slopes, F32), *tables, q, kb, vb, eqt, ek, lq, sg)


def _attn_cache_kernel(q_ref, kn_ref, vn_ref, ck_ref, cv_ref, lq_ref, sg_ref, o_ref, *,
                       lam_init, slopes):
    tn = q_ref.shape[1]
    past = ck_ref.shape[1]
    lam = _lambda(lq_ref, lam_init)

    def distance(k0, width):
        qpos = past + lax.broadcasted_iota(jnp.int32, (tn, width), 0)
        kpos = k0 + lax.broadcasted_iota(jnp.int32, (tn, width), 1)
        visible = (kpos // CHUNK) <= (qpos // CHUNK)
        return visible, jnp.abs(qpos - kpos).astype(F32)

    vis_c, dist_c = distance(0, past)
    vis_n, dist_n = distance(past, tn)

    for h, slope in enumerate(slopes):
        cols = slice(h * LANES, (h + 1) * LANES)
        q1, q2 = _split_maps(q_ref[0, :, cols])
        ck = ck_ref[0, :, cols].astype(BF16)
        cv = cv_ref[0, :, cols].astype(BF16)
        kn = kn_ref[0, :, cols]
        vn = vn_ref[0, :, cols]
        bias_c = jnp.where(vis_c, -slope * dist_c, NEG_BIG)
        bias_n = jnp.where(vis_n, -slope * dist_n, NEG_BIG)

        def one_map(a):
            sc = _qk(a, ck) + bias_c
            sn = _qk(a, kn) + bias_n
            m = jnp.maximum(jnp.max(sc, axis=-1, keepdims=True), jnp.max(sn, axis=-1, keepdims=True))
            pc = jnp.exp2(sc - m)
            pn = jnp.exp2(sn - m)
            l = jnp.sum(pc, axis=-1, keepdims=True) + jnp.sum(pn, axis=-1, keepdims=True)
            acc = (jnp.dot(pc.astype(BF16), cv, preferred_element_type=F32)
                   + jnp.dot(pn.astype(BF16), vn, preferred_element_type=F32))
            return l, acc

        l1, acc1 = one_map(q1)
        l2, acc2 = one_map(q2)
        o_ref[0, :, cols] = _finish_heads(acc1, l1, acc2, l2, lam, sg_ref[...],
                                          lam_init).astype(o_ref.dtype)


def _attention_cached(q, kb, vb, cache_k, cache_v, slopes, lq, sg, lam_init):
    b, tn, width = q.shape
    past = cache_k.shape[1]
    new_spec = pl.BlockSpec((1, tn, width), lambda bi: (bi, 0, 0))
    past_spec = pl.BlockSpec((1, past, width), lambda bi: (bi, 0, 0))
    return pl.pallas_call(
        functools.partial(_attn_cache_kernel, lam_init=lam_init,
                          slopes=tuple(float(s) for s in slopes)),
        grid=(b,),
        in_specs=[new_spec, new_spec, new_spec, past_spec, past_spec,
                  pl.BlockSpec(lq.shape, lambda bi: (0, 0)),
                  pl.BlockSpec(sg.shape, lambda bi: (0, 0))],
        out_specs=new_spec,
        out_shape=jax.ShapeDtypeStruct((b, tn, width), BF16),
        compiler_params=_params(1),
        name="diff_attention_cached",
    )(q, kb, vb, cache_k, cache_v, lq, sg)


def _out_kernel(x_ref, o_ref, ga_ref, gb_ref, mods_ref, ng_ref, wba_ref, wout_ref, wgu_ref, wd_ref,
                y_ref):
    x = x_ref[0]
    oa = jnp.dot(o_ref[0], wba_ref[...], preferred_element_type=F32)
    mixed = (ga_ref[0].astype(F32) * oa + gb_ref[0].astype(F32)).astype(BF16)
    x = x + mods_ref[0, 5] * jnp.dot(mixed, wout_ref[...], preferred_element_type=F32)
    h = _mod_norm(x, ng_ref[2:3, :], mods_ref, 6).astype(BF16)
    x = x + 0.5 * mods_ref[0, 8] * _swiglu(h, wgu_ref, wd_ref)
    y_ref[0] = _rms(x) * ng_ref[3:4, :]


def _mix_ffn2(x, o, ga, gb, mods, norm_g, wba, wout, wgu, wd, tm):
    g, t, d = x.shape
    row_spec, mods_spec = _row_specs(x, mods, tm)
    consts = (norm_g, wba, wout, wgu, wd)
    return pl.pallas_call(
        _out_kernel,
        grid=(g, t // tm),
        in_specs=[row_spec] * 4 + [mods_spec] + [_const_spec(a.shape) for a in consts],
        out_specs=row_spec,
        out_shape=jax.ShapeDtypeStruct((g, t, d), F32),
        compiler_params=_params(2),
        name="mix_ffn2",
    )(x, o, ga, gb, mods, *consts)


def _row_tile(t, target):
    tm = min(t, target)
    while t % tm:
        tm //= 2
    return tm


def kernel(x_prompt, x_sample, cache_k, cache_v, c_prompt, c_sample, ada_w, ada_b, norm_g, ffn1_wgu,
           ffn1_wd, w_in, q_norm_g, k_norm_g, lambda_qk, attn_subln_g, gmlp_vnorm_g, gmlp_ws, gmlp_bs,
           w_gate, b_gate, w_branch, w_out, ffn2_wgu, ffn2_wd):
    assert ada_w.shape[0] == 1, "single-layer step"
    bp, tp, d = x_prompt.shape
    bs_, ts, _ = x_sample.shape
    _, _, past, n_heads, _, hd = cache_k.shape
    assert n_heads == 8 and 2 * hd == LANES and n_heads * LANES == d
    lam_init = _lambda_init(0)
    log2e = math.log2(math.e)
    slopes = (2.0 ** (-8.0 * np.arange(1, n_heads + 1) / n_heads) * log2e).astype(np.float32)
    q_scale = float(hd) ** -0.5 * log2e

    assert ffn1_wd.shape[1] % FFN_CHUNK == 0 and w_in.shape[2] == 5 * d
    wgu1, wd1 = ffn1_wgu[0].astype(BF16), ffn1_wd[0].astype(BF16)
    wgu2, wd2 = ffn2_wgu[0].astype(BF16), ffn2_wd[0].astype(BF16)
    win = w_in[0].astype(BF16)
    wgate = w_gate[0].astype(BF16)
    bgate = b_gate[0].reshape(2, d)
    wba = w_branch[0, :d].astype(BF16)
    wbb = w_branch[0, d:].astype(BF16)
    wout = w_out[0].astype(BF16)
    ng = norm_g[0]
    qg = jnp.tile(q_norm_g[0], d // hd).reshape(1, d)
    kg = jnp.tile(k_norm_g[0], d // hd).reshape(1, d)
    n_groups = d // hd
    member = (np.arange(d)[:, None] // hd == np.arange(LANES)[None, :]).astype(np.float32)
    gsum = jnp.asarray(member / hd, BF16)
    gbcast = jnp.asarray(np.concatenate([member.T, member.T], axis=0), BF16)
    assert n_groups <= LANES
    vng = gmlp_vnorm_g[0].reshape(1, d)
    sg = attn_subln_g[0].reshape(1, LANES)
    lq = lambda_qk[0]

    mods = _mods(jnp.concatenate([c_prompt, c_sample], axis=0), ada_w[0], ada_b[0])
    mods = mods.reshape(bp + bs_, N_MODS, d)
    mods_p = mods[:bp].reshape(bp, N_MODS, 1, d)
    mods_s = jnp.repeat(mods[bp:].transpose(1, 0, 2), ts, axis=1).reshape(1, N_MODS, bs_ * ts, d)

    def gmlp_consts(t):
        rows = min(t, GMLP_CHUNK)
        ws = gmlp_ws[0][:, :rows, :rows]
        bs = jnp.repeat(gmlp_bs[0][:, :rows].T, d // GMLP_GROUPS, axis=1)
        return rows, ws, bs

    def layer(x, mods_x, seq, attend, tm_ffn, tm_proj, want_gv):
        _, t, _ = x.shape
        rows, ws, bs = gmlp_consts(seq)
        x1 = _ffn1(x, mods_x, ng, wgu1, wd1, _row_tile(t, tm_ffn))
        tmp = max(_row_tile(t, tm_proj), rows)
        outs = _proj(x1, mods_x, ng, win, wgate, bgate, qg, kg, gsum, gbcast, vng, ws, bs, wbb,
                     tmp, rows, q_scale, want_gv)
        q, k, v, kb, vb, ga, gb = outs[:7]
        o = attend(q, kb, vb)
        y = _mix_ffn2(x1, o, ga, gb, mods_x, ng, wba, wout, wgu2, wd2, _row_tile(t, tm_ffn))
        return y, k, v, (outs[7] if want_gv else None)

    y_p, k_p, v_p, _ = layer(
        x_prompt, mods_p, tp, lambda q, kb, vb: _attention(q, kb, vb, slopes, lq, sg, lam_init),
        512, 512, False)

    ck = cache_k[0].reshape(bs_, past, d)
    cv = cache_v[0].reshape(bs_, past, d)

    def attend_sample(q, kb, vb):
        shape = (bs_, ts, d)
        o = _attention_cached(q.reshape(shape), kb.reshape(shape), vb.reshape(shape), ck, cv,
                              slopes, lq, sg, lam_init)
        return o.reshape(1, bs_ * ts, d)

    y_s, k_s, v_s, gv_s = layer(
        x_sample.reshape(1, bs_ * ts, d), mods_s, ts, attend_sample, 512, 512, True)

    return (y_p, y_s.reshape(bs_, ts, d),
            k_p.reshape(1, bp, tp, n_heads, 2, hd), v_p.reshape(1, bp, tp, n_heads, 2 * hd),
            k_s.reshape(1, bs_, ts, n_heads, 2, hd), v_s.reshape(1, bs_, ts, n_heads, 2 * hd),
            gv_s.reshape(1, bs_, ts, d))
```

```python
import functools
import math

import numpy as np
import jax
import jax.numpy as jnp
from jax import lax
from jax.experimental import pallas as pl
from jax.experimental.pallas import tpu as pltpu

F32 = jnp.float32
BF16 = jnp.bfloat16

EPS = 1e-6
CHUNK = 64
GMLP_CHUNK = 128
GMLP_GROUPS = 8
N_MODS = 9
NEG_BIG = -1e30

V7X_VMEM_BYTES = 64 * 1024 * 1024
VMEM_LIMIT_BYTES = V7X_VMEM_BYTES - 8 * 1024 * 1024
LANES = 128
MXU_DIM = 256
ATTN_SUB = 2 * MXU_DIM
ATTN_TQ = 512
FFN_CHUNK = 256
ALIBI_SPLIT = 256
ALIBI_PERIOD = 1024


def _lambda_init(layer_idx):
    return 0.8 - 0.6 * math.exp(-0.3 * layer_idx)


def _const_spec(shape):
    nd = len(shape)
    return pl.BlockSpec(shape, lambda *_: (0,) * nd, pipeline_mode=pl.Buffered(1))


def _params(n_axes):
    return pltpu.CompilerParams(
        dimension_semantics=("arbitrary",) * n_axes, vmem_limit_bytes=VMEM_LIMIT_BYTES)


def _mods_kernel(c_ref, w_ref, b_ref, o_ref):
    c = c_ref[...]
    h = c * jax.nn.sigmoid(c)
    o_ref[...] = jnp.dot(h, w_ref[...], preferred_element_type=F32,
                         precision=lax.Precision.HIGHEST) + b_ref[...]


def _mods(c_all, ada_w, ada_b):
    n, d = c_all.shape
    width = ada_w.shape[1]
    tn = d
    return pl.pallas_call(
        _mods_kernel,
        grid=(width // tn,),
        in_specs=[pl.BlockSpec((n, d), lambda j: (0, 0)),
                  pl.BlockSpec((d, tn), lambda j: (0, j)),
                  pl.BlockSpec((1, tn), lambda j: (0, j))],
        out_specs=pl.BlockSpec((n, tn), lambda j: (0, j)),
        out_shape=jax.ShapeDtypeStruct((n, width), F32),
        compiler_params=_params(1),
        name="adaln_mods",
    )(c_all, ada_w, ada_b.reshape(1, width))


def _rms(x):
    return x * lax.rsqrt(jnp.mean(x * x, axis=-1, keepdims=True) + EPS)


def _mod_norm(x, g, mods_ref, first):
    return (_rms(x) * g) * (1.0 + mods_ref[0, first + 1]) + mods_ref[0, first]


def _swiglu(h, wgu_ref, wd_ref):
    f = wd_ref.shape[0]
    acc = None
    for c0 in range(0, f, FFN_CHUNK):
        gate = jnp.dot(h, wgu_ref[:, c0:c0 + FFN_CHUNK], preferred_element_type=F32)
        up = jnp.dot(h, wgu_ref[:, f + c0:f + c0 + FFN_CHUNK], preferred_element_type=F32)
        a = (gate * jax.nn.sigmoid(gate) * up).astype(BF16)
        part = jnp.dot(a, wd_ref[c0:c0 + FFN_CHUNK, :], preferred_element_type=F32)
        acc = part if acc is None else acc + part
    return acc


def _gelu_tanh(x):
    return 0.5 * x * (1.0 + jnp.tanh(math.sqrt(2.0 / math.pi) * (x + 0.044715 * (x * x * x))))


def _row_specs(x, mods, tm):
    _, _, d = x.shape
    r = mods.shape[2]
    row_spec = pl.BlockSpec((1, tm, d), lambda g, i: (g, i, 0))
    if r == 1:
        mods_spec = pl.BlockSpec((1, N_MODS, 1, d), lambda g, i: (g, 0, 0, 0))
    else:
        mods_spec = pl.BlockSpec((1, N_MODS, tm, d), lambda g, i: (g, 0, i, 0))
    return row_spec, mods_spec


def _ffn_kernel(x_ref, mods_ref, ng_ref, wgu_ref, wd_ref, o_ref):
    x = x_ref[0]
    h = _mod_norm(x, ng_ref[0:1, :], mods_ref, 0).astype(BF16)
    o_ref[0] = x + 0.5 * mods_ref[0, 2] * _swiglu(h, wgu_ref, wd_ref)


def _ffn1(x, mods, norm_g, wgu, wd, tm):
    g, t, d = x.shape
    row_spec, mods_spec = _row_specs(x, mods, tm)
    return pl.pallas_call(
        _ffn_kernel,
        grid=(g, t // tm),
        in_specs=[row_spec, mods_spec,
                  _const_spec(norm_g.shape), _const_spec(wgu.shape), _const_spec(wd.shape)],
        out_specs=row_spec,
        out_shape=jax.ShapeDtypeStruct((g, t, d), F32),
        compiler_params=_params(2),
        name="ffn1",
    )(x, mods, norm_g, wgu, wd)


def _proj_kernel(x_ref, mods_ref, ng_ref, win_ref, wgate_ref, bgate_ref, qg_ref, kg_ref, gsum_ref,
                 gbcast_ref, vng_ref, ws_ref, bs_ref, wbb_ref,
                 q_o, k_o, v_o, kb_o, vb_o, ga_o, gb_o, *maybe_gv_o, rows, q_scale):
    x = x_ref[0]
    tm, d = x.shape
    h = _mod_norm(x, ng_ref[1:2, :], mods_ref, 3).astype(BF16)

    def w_in(j):
        return win_ref[:, j * d:(j + 1) * d]

    def group_norm(z, g):
        ms = jnp.dot((z * z).astype(BF16), gsum_ref[...], preferred_element_type=F32)
        r = lax.rsqrt(ms + EPS)
        r_hi = r.astype(BF16)
        r_lo = (r - r_hi.astype(F32)).astype(BF16)
        scale = jnp.dot(jnp.concatenate([r_hi, r_lo], axis=1), gbcast_ref[...],
                        preferred_element_type=F32)
        return z * scale * g

    q = group_norm(jnp.dot(h, w_in(0), preferred_element_type=F32), qg_ref[...])
    q_o[0] = (q * q_scale).astype(BF16)
    k = group_norm(jnp.dot(h, w_in(1), preferred_element_type=F32), kg_ref[...])
    k_o[0] = k
    kb_o[0] = k.astype(BF16)
    v = jnp.dot(h, w_in(2), preferred_element_type=F32)
    v_o[0] = v
    vb_o[0] = v.astype(BF16)

    u = _gelu_tanh(jnp.dot(h, w_in(3), preferred_element_type=F32))
    gz = _gelu_tanh(jnp.dot(h, w_in(4), preferred_element_type=F32))
    gv = _rms(gz) * vng_ref[...]
    if maybe_gv_o:
        maybe_gv_o[0][0] = gv
    gvb = gv.astype(BF16)

    row_i = lax.broadcasted_iota(jnp.int32, (rows, rows), 0)
    col_i = lax.broadcasted_iota(jnp.int32, (rows, rows), 1)
    tril = col_i <= row_i
    gw = d // GMLP_GROUPS
    ws = [jnp.where(tril, ws_ref[g], 0.0).astype(BF16) for g in range(GMLP_GROUPS)]
    s_chunks = []
    for c in range(tm // rows):
        r0 = c * rows
        mix = jnp.concatenate(
            [jnp.dot(ws[g], gvb[r0:r0 + rows, g * gw:(g + 1) * gw], preferred_element_type=F32)
             for g in range(GMLP_GROUPS)], axis=1)
        s_chunks.append(u[r0:r0 + rows, :] * (mix + bs_ref[...]))
    s = s_chunks[0] if len(s_chunks) == 1 else jnp.concatenate(s_chunks, axis=0)
    sb = jnp.dot(s.astype(BF16), wbb_ref[...], preferred_element_type=F32)

    gate_a = jax.nn.sigmoid(
        jnp.dot(h, wgate_ref[:, 0:d], preferred_element_type=F32) + bgate_ref[0:1, :])
    gate_b = jax.nn.sigmoid(
        jnp.dot(h, wgate_ref[:, d:2 * d], preferred_element_type=F32) + bgate_ref[1:2, :])
    ga_o[0] = gate_a.astype(BF16)
    gb_o[0] = (gate_b * sb).astype(BF16)


def _proj(x, mods, norm_g, win, wgate, bgate, qg, kg, gsum, gbcast, vng, ws, bs, wbb, tm, rows,
          q_scale, want_gv):
    g, t, d = x.shape
    row_spec, mods_spec = _row_specs(x, mods, tm)
    out_shapes = [jax.ShapeDtypeStruct((g, t, d), BF16),
                  jax.ShapeDtypeStruct((g, t, d), F32),
                  jax.ShapeDtypeStruct((g, t, d), F32),
                  jax.ShapeDtypeStruct((g, t, d), BF16),
                  jax.ShapeDtypeStruct((g, t, d), BF16),
                  jax.ShapeDtypeStruct((g, t, d), BF16),
                  jax.ShapeDtypeStruct((g, t, d), BF16)]
    if want_gv:
        out_shapes.append(jax.ShapeDtypeStruct((g, t, d), F32))
    consts = (norm_g, win, wgate, bgate, qg, kg, gsum, gbcast, vng, ws, bs, wbb)
    return pl.pallas_call(
        functools.partial(_proj_kernel, rows=rows, q_scale=q_scale),
        grid=(g, t // tm),
        in_specs=[row_spec, mods_spec] + [_const_spec(a.shape) for a in consts],
        out_specs=[row_spec] * len(out_shapes),
        out_shape=out_shapes,
        compiler_params=_params(2),
        name="mixer_proj",
    )(x, mods, *consts)


def _split_maps(q):
    lane = lax.broadcasted_iota(jnp.int32, q.shape, 1)
    half = q.shape[1] // 2
    zero = jnp.zeros_like(q)
    return jnp.where(lane < half, q, zero), jnp.where(lane >= half, q, zero)


def _qk(a, k):
    return lax.dot_general(a, k, (((1,), (1,)), ((), ())), preferred_element_type=F32)


def _lambda(lq_ref, lam_init):
    lq = lq_ref[...]
    t1 = jnp.sum(lq[0:1, :] * lq[1:2, :], axis=-1, keepdims=True)
    t2 = jnp.sum(lq[2:3, :] * lq[3:4, :], axis=-1, keepdims=True)
    return jnp.exp(t1) - jnp.exp(t2) + lam_init


def _finish_heads(acc1, l1, acc2, l2, lam, sg, lam_init):
    o = acc1 / l1 - lam * (acc2 / l2)
    return (_rms(o) * sg) * (1.0 - lam_init)


def _attn_kernel(slope_ref, tile_ref, period_ref, bias_ref, first_ref, last_ref, dist_ref,
                 q_ref, k_ref, v_ref, eqt_ref, ek_ref, lq_ref, sg_ref, o_ref,
                 kx_scr, vt_scr, bias_scr, at_scr, s_scr, p_last_scr, smax_scr, m_scr, fin_scr,
                 acc_scr, *, lam_init, wide, tq, n_steps):
    h = pl.program_id(0)
    t_all = k_ref.shape[1]
    slope = slope_ref[h]
    per_wide = wide // tq

    @pl.when(pl.program_id(1) == 0)
    def _():
        for c in range(t_all // wide):
            kx_scr[c * wide:(c + 1) * wide, LANES:2 * LANES] = ek_ref[0]
        vt_scr[LANES:, :] = jnp.ones((vt_scr.shape[0] - LANES, t_all), BF16)
        s_pos = lax.broadcasted_iota(jnp.int32, (wide, tq), 0)
        for c in range(per_wide):
            t_pos = c * tq + lax.broadcasted_iota(jnp.int32, (wide, tq), 1)
            visible = (s_pos // CHUNK) <= (t_pos // CHUNK)
            fix = (jnp.abs(t_pos - s_pos) - (t_pos - s_pos)).astype(F32)
            bias_scr[c] = jnp.where(visible, -slope * fix, NEG_BIG)

    kx_scr[:, 0:LANES] = k_ref[0]
    for c in range(t_all // wide):
        vt_scr[0:LANES, c * wide:(c + 1) * wide] = (
            v_ref[0, c * wide:(c + 1) * wide, :].astype(F32).T.astype(BF16))
    p_last_scr[...] = jnp.zeros(p_last_scr.shape, BF16)
    acc_scr[...] = jnp.zeros(acc_scr.shape, F32)
    m_scr[...] = jnp.full(m_scr.shape, NEG_BIG, F32)

    def build_queries(tile):
        r0 = pl.multiple_of(tile * tq, tq)
        qt = q_ref[0, pl.ds(r0, tq), :].astype(F32).T
        row = lax.broadcasted_iota(jnp.int32, qt.shape, 0)
        zero = jnp.zeros_like(qt)
        at_scr[0, 0:LANES, :] = jnp.where(row < LANES // 2, qt, zero).astype(BF16)
        at_scr[1, 0:LANES, :] = jnp.where(row >= LANES // 2, qt, zero).astype(BF16)
        alibi_rows = eqt_ref[0, tile % per_wide]
        at_scr[0, LANES:2 * LANES, :] = alibi_rows
        at_scr[1, LANES:2 * LANES, :] = alibi_rows

    sub = p_last_scr.shape[1]
    n_sub = wide // sub

    def scores(e, slot, masked):
        kx = kx_scr[pl.ds(pl.multiple_of(period_ref[e] * wide, wide), wide), :]
        for mp in range(2):
            s = jnp.dot(kx, at_scr[mp], preferred_element_type=F32)
            if masked:
                s = s + bias_scr[bias_ref[e]]
            s_scr[slot, mp] = s
            smax_scr[slot, mp, 0:1, :] = jnp.max(s, axis=0, keepdims=True)

    def values_block(e, c):
        off = pl.multiple_of(period_ref[e] * wide + c * sub, sub)
        return vt_scr[:, pl.ds(off, sub)]

    def pipeline_step(e, slot, masked):
        shift = -slope * (dist_ref[e] * wide).astype(F32)
        is_first = first_ref[e] == 1
        offsets, alphas = [], []
        for mp in range(2):
            m_old = jnp.where(is_first, NEG_BIG, m_scr[mp, 0:1, :])
            m_new = jnp.maximum(m_old, smax_scr[slot, mp, 0:1, :] + shift)
            alphas.append(jnp.exp2(m_old - m_new))
            m_scr[mp, 0:1, :] = m_new
            offsets.append(m_new - shift)
        vt_pending = values_block(e - 1, n_sub - 1)
        pending = [jnp.dot(vt_pending, p_last_scr[mp], preferred_element_type=F32) for mp in range(2)]
        scores(e + 1, 1 - slot, masked)
        pv = [None, None]
        for c in range(n_sub):
            rows = slice(c * sub, (c + 1) * sub)
            for mp in range(2):
                p = jnp.exp2(s_scr[slot, mp, rows, :] - offsets[mp]).astype(BF16)
                if c == n_sub - 1:
                    p_last_scr[mp] = p
                else:
                    part = jnp.dot(values_block(e, c), p, preferred_element_type=F32)
                    pv[mp] = part if pv[mp] is None else pv[mp] + part
        for mp in range(2):
            done = acc_scr[mp] + pending[mp]
            fin_scr[mp] = done
            acc_scr[mp] = alphas[mp] * done + pv[mp]

    def finalize(tile):
        lam = _lambda(lq_ref, lam_init)
        a1 = fin_scr[0]
        a2 = fin_scr[1]
        inv1 = 1.0 / a1[LANES:LANES + 1]
        inv2 = lam / a2[LANES:LANES + 1]
        ot = a1[0:LANES] * inv1 - a2[0:LANES] * inv2
        ot = ot * lax.rsqrt(jnp.mean(ot * ot, axis=0, keepdims=True) + EPS)
        r0 = pl.multiple_of(tile * tq, tq)
        o_ref[0, pl.ds(r0, tq), :] = ((ot.T * sg_ref[...]) * (1.0 - lam_init)).astype(o_ref.dtype)

    @pl.when(first_ref[1] == 1)
    def _():
        build_queries(tile_ref[1])

    @pl.when(first_ref[1] == 1)
    def _():
        scores(1, 0, True)

    def iteration(k, carry):
        e = k + 1

        @pl.when(first_ref[e + 1] == 1)
        def _():
            build_queries(tile_ref[e + 1])

        next_masked = bias_ref[e + 1] < per_wide
        for slot in (0, 1):
            for masked in (False, True):
                cond = next_masked if masked else jnp.logical_not(next_masked)

                @pl.when(jnp.logical_and(k % 2 == slot, cond))
                def _():
                    pipeline_step(e, slot, masked)

        @pl.when(last_ref[e - 1] == 1)
        def _():
            finalize(tile_ref[e - 1])

        return carry

    lax.fori_loop(0, n_steps + 1, iteration, 0)


def _bf16_pieces(x, n):
    pieces, rest = [], np.asarray(x, np.float32)
    for _ in range(n):
        p = rest.astype(BF16).astype(np.float32)
        pieces.append(p)
        rest = rest - p
    return pieces


def _alibi_rows_cols(slopes, period):
    n_heads = slopes.shape[0]
    pos = np.arange(period)
    offs = ((pos // ALIBI_SPLIT) * ALIBI_SPLIT, pos % ALIBI_SPLIT)
    eqt = np.zeros((n_heads, LANES, period), np.float32)
    ek = np.zeros((n_heads, period, LANES), np.float32)
    lane = 0
    for piece in _bf16_pieces(slopes, 3):
        for off in offs:
            eqt[:, lane, :] = off[None, :]
            ek[:, :, lane] = -piece[:, None]
            eqt[:, lane + 1, :] = piece[:, None]
            ek[:, :, lane + 1] = off[None, :]
            lane += 2
    return jnp.asarray(eqt, BF16), jnp.asarray(ek, BF16)


def _attention_schedule(n_tiles, per_wide):
    tile, period, bias, first, last, dist = [], [], [], [], [], []
    for i in range(n_tiles):
        n_wide = i // per_wide
        for j in range(n_wide + 1):
            own = j == n_wide
            tile.append(i)
            period.append(j)
            bias.append(i % per_wide if own else per_wide)
            first.append(int(j == 0))
            last.append(int(own))
            dist.append(n_wide - j)
    pad = lambda v, fill: np.asarray([fill] + v + [fill, fill], np.int32)
    return (pad(tile, n_tiles - 1), pad(period, 0), pad(bias, per_wide), pad(first, 1),
            pad(last, 0), pad(dist, 0)), len(tile)


def _attention(q, kb, vb, slopes, lq, sg, lam_init):
    b, t, width = q.shape
    n_heads = width // LANES
    tq = min(ATTN_TQ, t)
    wide = min(ALIBI_PERIOD, t)
    assert t % wide == 0 and wide % tq == 0
    per_wide = wide // tq
    eqt, ek = _alibi_rows_cols(slopes, wide)
    eqt = eqt.reshape(n_heads, LANES, per_wide, tq).transpose(0, 2, 1, 3)
    tables, n_steps = _attention_schedule(t // tq, per_wide)
    ones_rows = 16
    n_prefetch = 1 + len(tables)
    head_spec = pl.BlockSpec((1, t, LANES), lambda h, bi, *_: (bi, 0, h))
    grid_spec = pltpu.PrefetchScalarGridSpec(
        num_scalar_prefetch=n_prefetch,
        grid=(n_heads, b),
        in_specs=[head_spec, head_spec, head_spec,
                  pl.BlockSpec((1, per_wide, LANES, tq), lambda h, bi, *_: (h, 0, 0, 0)),
                  pl.BlockSpec((1, wide, LANES), lambda h, bi, *_: (h, 0, 0)),
                  pl.BlockSpec(lq.shape, lambda h, bi, *_: (0, 0)),
                  pl.BlockSpec(sg.shape, lambda h, bi, *_: (0, 0))],
        out_specs=head_spec,
        scratch_shapes=[pltpu.VMEM((t, 2 * LANES), BF16),
                        pltpu.VMEM((LANES + ones_rows, t), BF16),
                        pltpu.VMEM((per_wide, wide, tq), F32),
                        pltpu.VMEM((2, 2 * LANES, tq), BF16),
                        pltpu.VMEM((2, 2, wide, tq), F32),
                        pltpu.VMEM((2, ATTN_SUB, tq), BF16),
                        pltpu.VMEM((2, 2, 8, tq), F32),
                        pltpu.VMEM((2, 8, tq), F32),
                        pltpu.VMEM((2, LANES + ones_rows, tq), F32),
                        pltpu.VMEM((2, LANES + ones_rows, tq), F32)],
    )
    return pl.pallas_call(
        functools.partial(_attn_kernel, lam_init=lam_init, wide=wide, tq=tq, n_steps=n_steps),
        grid_spec=grid_spec,
        out_shape=jax.ShapeDtypeStruct((b, t, width), BF16),
        compiler_params=_params(2),
        name="diff_attention",
    )(jnp.asarray(slopes, F32), *tables, q, kb, vb, eqt, ek, lq, sg)


def _attn_cache_kernel(q_ref, kn_ref, vn_ref, ck_ref, cv_ref, lq_ref, sg_ref, o_ref, *,
                       lam_init, slopes):
    tn = q_ref.shape[1]
    past = ck_ref.shape[1]
    lam = _lambda(lq_ref, lam_init)

    def distance(k0, width):
        qpos = past + lax.broadcasted_iota(jnp.int32, (tn, width), 0)
        kpos = k0 + lax.broadcasted_iota(jnp.int32, (tn, width), 1)
        visible = (kpos // CHUNK) <= (qpos // CHUNK)
        return visible, jnp.abs(qpos - kpos).astype(F32)

    vis_c, dist_c = distance(0, past)
    vis_n, dist_n = distance(past, tn)

    for h, slope in enumerate(slopes):
        cols = slice(h * LANES, (h + 1) * LANES)
        q1, q2 = _split_maps(q_ref[0, :, cols])
        ck = ck_ref[0, :, cols].astype(BF16)
        cv = cv_ref[0, :, cols].astype(BF16)
        kn = kn_ref[0, :, cols]
        vn = vn_ref[0, :, cols]
        bias_c = jnp.where(vis_c, -slope * dist_c, NEG_BIG)
        bias_n = jnp.where(vis_n, -slope * dist_n, NEG_BIG)

        def one_map(a):
            sc = _qk(a, ck) + bias_c
            sn = _qk(a, kn) + bias_n
            m = jnp.maximum(jnp.max(sc, axis=-1, keepdims=True), jnp.max(sn, axis=-1, keepdims=True))
            pc = jnp.exp2(sc - m)
            pn = jnp.exp2(sn - m)
            l = jnp.sum(pc, axis=-1, keepdims=True) + jnp.sum(pn, axis=-1, keepdims=True)
            acc = (jnp.dot(pc.astype(BF16), cv, preferred_element_type=F32)
                   + jnp.dot(pn.astype(BF16), vn, preferred_element_type=F32))
            return l, acc

        l1, acc1 = one_map(q1)
        l2, acc2 = one_map(q2)
        o_ref[0, :, cols] = _finish_heads(acc1, l1, acc2, l2, lam, sg_ref[...],
                                          lam_init).astype(o_ref.dtype)


def _attention_cached(q, kb, vb, cache_k, cache_v, slopes, lq, sg, lam_init):
    b, tn, width = q.shape
    past = cache_k.shape[1]
    new_spec = pl.BlockSpec((1, tn, width), lambda bi: (bi, 0, 0))
    past_spec = pl.BlockSpec((1, past, width), lambda bi: (bi, 0, 0))
    return pl.pallas_call(
        functools.partial(_attn_cache_kernel, lam_init=lam_init,
                          slopes=tuple(float(s) for s in slopes)),
        grid=(b,),
        in_specs=[new_spec, new_spec, new_spec, past_spec, past_spec,
                  pl.BlockSpec(lq.shape, lambda bi: (0, 0)),
                  pl.BlockSpec(sg.shape, lambda bi: (0, 0))],
        out_specs=new_spec,
        out_shape=jax.ShapeDtypeStruct((b, tn, width), BF16),
        compiler_params=_params(1),
        name="diff_attention_cached",
    )(q, kb, vb, cache_k, cache_v, lq, sg)


def _out_kernel(x_ref, o_ref, ga_ref, gb_ref, mods_ref, ng_ref, wba_ref, wout_ref, wgu_ref, wd_ref,
                y_ref):
    x = x_ref[0]
    oa = jnp.dot(o_ref[0], wba_ref[...], preferred_element_type=F32)
    mixed = (ga_ref[0].astype(F32) * oa + gb_ref[0].astype(F32)).astype(BF16)
    x = x + mods_ref[0, 5] * jnp.dot(mixed, wout_ref[...], preferred_element_type=F32)
    h = _mod_norm(x, ng_ref[2:3, :], mods_ref, 6).astype(BF16)
    x = x + 0.5 * mods_ref[0, 8] * _swiglu(h, wgu_ref, wd_ref)
    y_ref[0] = _rms(x) * ng_ref[3:4, :]


def _mix_ffn2(x, o, ga, gb, mods, norm_g, wba, wout, wgu, wd, tm):
    g, t, d = x.shape
    row_spec, mods_spec = _row_specs(x, mods, tm)
    consts = (norm_g, wba, wout, wgu, wd)
    return pl.pallas_call(
        _out_kernel,
        grid=(g, t // tm),
        in_specs=[row_spec] * 4 + [mods_spec] + [_const_spec(a.shape) for a in consts],
        out_specs=row_spec,
        out_shape=jax.ShapeDtypeStruct((g, t, d), F32),
        compiler_params=_params(2),
        name="mix_ffn2",
    )(x, o, ga, gb, mods, *consts)


def _row_tile(t, target):
    tm = min(t, target)
    while t % tm:
        tm //= 2
    return tm


def kernel(x_prompt, x_sample, cache_k, cache_v, c_prompt, c_sample, ada_w, ada_b, norm_g, ffn1_wgu,
           ffn1_wd, w_in, q_norm_g, k_norm_g, lambda_qk, attn_subln_g, gmlp_vnorm_g, gmlp_ws, gmlp_bs,
           w_gate, b_gate, w_branch, w_out, ffn2_wgu, ffn2_wd):
    assert ada_w.shape[0] == 1, "single-layer step"
    bp, tp, d = x_prompt.shape
    bs_, ts, _ = x_sample.shape
    _, _, past, n_heads, _, hd = cache_k.shape
    assert n_heads == 8 and 2 * hd == LANES and n_heads * LANES == d
    lam_init = _lambda_init(0)
    log2e = math.log2(math.e)
    slopes = (2.0 ** (-8.0 * np.arange(1, n_heads + 1) / n_heads) * log2e).astype(np.float32)
    q_scale = float(hd) ** -0.5 * log2e

    assert ffn1_wd.shape[1] % FFN_CHUNK == 0 and w_in.shape[2] == 5 * d
    wgu1, wd1 = ffn1_wgu[0].astype(BF16), ffn1_wd[0].astype(BF16)
    wgu2, wd2 = ffn2_wgu[0].astype(BF16), ffn2_wd[0].astype(BF16)
    win = w_in[0].astype(BF16)
    wgate = w_gate[0].astype(BF16)
    bgate = b_gate[0].reshape(2, d)
    wba = w_branch[0, :d].astype(BF16)
    wbb = w_branch[0, d:].astype(BF16)
    wout = w_out[0].astype(BF16)
    ng = norm_g[0]
    qg = jnp.tile(q_norm_g[0], d // hd).reshape(1, d)
    kg = jnp.tile(k_norm_g[0], d // hd).reshape(1, d)
    n_groups = d // hd
    member = (np.arange(d)[:, None] // hd == np.arange(LANES)[None, :]).astype(np.float32)
    gsum = jnp.asarray(member / hd, BF16)
    gbcast = jnp.asarray(np.concatenate([member.T, member.T], axis=0), BF16)
    assert n_groups <= LANES
    vng = gmlp_vnorm_g[0].reshape(1, d)
    sg = attn_subln_g[0].reshape(1, LANES)
    lq = lambda_qk[0]

    mods = _mods(jnp.concatenate([c_prompt, c_sample], axis=0), ada_w[0], ada_b[0])
    mods = mods.reshape(bp + bs_, N_MODS, d)
    mods_p = mods[:bp].reshape(bp, N_MODS, 1, d)
    mods_s = jnp.repeat(mods[bp:].transpose(1, 0, 2), ts, axis=1).reshape(1, N_MODS, bs_ * ts, d)

    def gmlp_consts(t):
        rows = min(t, GMLP_CHUNK)
        ws = gmlp_ws[0][:, :rows, :rows]
        bs = jnp.repeat(gmlp_bs[0][:, :rows].T, d // GMLP_GROUPS, axis=1)
        return rows, ws, bs

    def layer(x, mods_x, seq, attend, tm_ffn, tm_proj, want_gv):
        _, t, _ = x.shape
        rows, ws, bs = gmlp_consts(seq)
        x1 = _ffn1(x, mods_x, ng, wgu1, wd1, _row_tile(t, tm_ffn))
        tmp = max(_row_tile(t, tm_proj), rows)
        outs = _proj(x1, mods_x, ng, win, wgate, bgate, qg, kg, gsum, gbcast, vng, ws, bs, wbb,
                     tmp, rows, q_scale, want_gv)
        q, k, v, kb, vb, ga, gb = outs[:7]
        o = attend(q, kb, vb)
        y = _mix_ffn2(x1, o, ga, gb, mods_x, ng, wba, wout, wgu2, wd2, _row_tile(t, tm_ffn))
        return y, k, v, (outs[7] if want_gv else None)

    y_p, k_p, v_p, _ = layer(
        x_prompt, mods_p, tp, lambda q, kb, vb: _attention(q, kb, vb, slopes, lq, sg, lam_init),
        512, 512, False)

    ck = cache_k[0].reshape(bs_, past, d)
    cv = cache_v[0].reshape(bs_, past, d)

    def attend_sample(q, kb, vb):
        shape = (bs_, ts, d)
        o = _attention_cached(q.reshape(shape), kb.reshape(shape), vb.reshape(shape), ck, cv,
                              slopes, lq, sg, lam_init)
        return o.reshape(1, bs_ * ts, d)

    y_s, k_s, v_s, gv_s = layer(
        x_sample.reshape(1, bs_ * ts, d), mods_s, ts, attend_sample, 512, 512, True)

    return (y_p, y_s.reshape(bs_, ts, d),
            k_p.reshape(1, bp, tp, n_heads, 2, hd), v_p.reshape(1, bp, tp, n_heads, 2 * hd),
            k_s.reshape(1, bs_, ts, n_heads, 2, hd), v_s.reshape(1, bs_, ts, n_heads, 2 * hd),
            gv_s.reshape(1, bs_, ts, d))
```

```python
import functools
import math

import numpy as np
import jax
import jax.numpy as jnp
from jax import lax
from jax.experimental import pallas as pl
from jax.experimental.pallas import tpu as pltpu

F32 = jnp.float32
BF16 = jnp.bfloat16

EPS = 1e-6
CHUNK = 64
GMLP_CHUNK = 128
GMLP_GROUPS = 8
N_MODS = 9
NEG_BIG = -1e30

V7X_VMEM_BYTES = 64 * 1024 * 1024
VMEM_LIMIT_BYTES = V7X_VMEM_BYTES - 8 * 1024 * 1024
LANES = 128
MXU_DIM = 256
ATTN_SUB = 2 * MXU_DIM
ATTN_TQ = 1024
FFN_CHUNK = 256
ALIBI_SPLIT = 256
ALIBI_PERIOD = 1024


def _lambda_init(layer_idx):
    return 0.8 - 0.6 * math.exp(-0.3 * layer_idx)


def _const_spec(shape):
    nd = len(shape)
    return pl.BlockSpec(shape, lambda *_: (0,) * nd, pipeline_mode=pl.Buffered(1))


def _params(n_axes):
    return pltpu.CompilerParams(
        dimension_semantics=("arbitrary",) * n_axes, vmem_limit_bytes=VMEM_LIMIT_BYTES)


def _mods_kernel(c_ref, w_ref, b_ref, o_ref):
    c = c_ref[...]
    h = c * jax.nn.sigmoid(c)
    o_ref[...] = jnp.dot(h, w_ref[...], preferred_element_type=F32,
                         precision=lax.Precision.HIGHEST) + b_ref[...]


def _mods(c_all, ada_w, ada_b):
    n, d = c_all.shape
    width = ada_w.shape[1]
    tn = d
    return pl.pallas_call(
        _mods_kernel,
        grid=(width // tn,),
        in_specs=[pl.BlockSpec((n, d), lambda j: (0, 0)),
                  pl.BlockSpec((d, tn), lambda j: (0, j)),
                  pl.BlockSpec((1, tn), lambda j: (0, j))],
        out_specs=pl.BlockSpec((n, tn), lambda j: (0, j)),
        out_shape=jax.ShapeDtypeStruct((n, width), F32),
        compiler_params=_params(1),
        name="adaln_mods",
    )(c_all, ada_w, ada_b.reshape(1, width))


def _rms(x):
    return x * lax.rsqrt(jnp.mean(x * x, axis=-1, keepdims=True) + EPS)


def _mod_norm(x, g, mods_ref, first):
    return (_rms(x) * g) * (1.0 + mods_ref[0, first + 1]) + mods_ref[0, first]


def _swiglu(h, wgu_ref, wd_ref):
    f = wd_ref.shape[0]
    acc = None
    for c0 in range(0, f, FFN_CHUNK):
        gate = jnp.dot(h, wgu_ref[:, c0:c0 + FFN_CHUNK], preferred_element_type=F32)
        up = jnp.dot(h, wgu_ref[:, f + c0:f + c0 + FFN_CHUNK], preferred_element_type=F32)
        a = (gate * jax.nn.sigmoid(gate) * up).astype(BF16)
        part = jnp.dot(a, wd_ref[c0:c0 + FFN_CHUNK, :], preferred_element_type=F32)
        acc = part if acc is None else acc + part
    return acc


def _gelu_tanh(x):
    return 0.5 * x * (1.0 + jnp.tanh(math.sqrt(2.0 / math.pi) * (x + 0.044715 * (x * x * x))))


def _row_specs(x, mods, tm):
    _, _, d = x.shape
    r = mods.shape[2]
    row_spec = pl.BlockSpec((1, tm, d), lambda g, i: (g, i, 0))
    if r == 1:
        mods_spec = pl.BlockSpec((1, N_MODS, 1, d), lambda g, i: (g, 0, 0, 0))
    else:
        mods_spec = pl.BlockSpec((1, N_MODS, tm, d), lambda g, i: (g, 0, i, 0))
    return row_spec, mods_spec


def _ffn_kernel(x_ref, mods_ref, ng_ref, wgu_ref, wd_ref, o_ref):
    x = x_ref[0]
    h = _mod_norm(x, ng_ref[0:1, :], mods_ref, 0).astype(BF16)
    o_ref[0] = x + 0.5 * mods_ref[0, 2] * _swiglu(h, wgu_ref, wd_ref)


def _ffn1(x, mods, norm_g, wgu, wd, tm):
    g, t, d = x.shape
    row_spec, mods_spec = _row_specs(x, mods, tm)
    return pl.pallas_call(
        _ffn_kernel,
        grid=(g, t // tm),
        in_specs=[row_spec, mods_spec,
                  _const_spec(norm_g.shape), _const_spec(wgu.shape), _const_spec(wd.shape)],
        out_specs=row_spec,
        out_shape=jax.ShapeDtypeStruct((g, t, d), F32),
        compiler_params=_params(2),
        name="ffn1",
    )(x, mods, norm_g, wgu, wd)


def _proj_kernel(x_ref, mods_ref, ng_ref, win_ref, wgate_ref, bgate_ref, qg_ref, kg_ref, gsum_ref,
                 gbcast_ref, vng_ref, ws_ref, bs_ref, wbb_ref,
                 q_o, k_o, v_o, kb_o, vb_o, ga_o, gb_o, *maybe_gv_o, rows, q_scale):
    x = x_ref[0]
    tm, d = x.shape
    h = _mod_norm(x, ng_ref[1:2, :], mods_ref, 3).astype(BF16)

    def w_in(j):
        return win_ref[:, j * d:(j + 1) * d]

    def group_norm(z, g):
        ms = jnp.dot((z * z).astype(BF16), gsum_ref[...], preferred_element_type=F32)
        r = lax.rsqrt(ms + EPS)
        r_hi = r.astype(BF16)
        r_lo = (r - r_hi.astype(F32)).astype(BF16)
        scale = jnp.dot(jnp.concatenate([r_hi, r_lo], axis=1), gbcast_ref[...],
                        preferred_element_type=F32)
        return z * scale * g

    q = group_norm(jnp.dot(h, w_in(0), preferred_element_type=F32), qg_ref[...])
    q_o[0] = (q * q_scale).astype(BF16)
    k = group_norm(jnp.dot(h, w_in(1), preferred_element_type=F32), kg_ref[...])
    k_o[0] = k
    kb_o[0] = k.astype(BF16)
    v = jnp.dot(h, w_in(2), preferred_element_type=F32)
    v_o[0] = v
    vb_o[0] = v.astype(BF16)

    u = _gelu_tanh(jnp.dot(h, w_in(3), preferred_element_type=F32))
    gz = _gelu_tanh(jnp.dot(h, w_in(4), preferred_element_type=F32))
    gv = _rms(gz) * vng_ref[...]
    if maybe_gv_o:
        maybe_gv_o[0][0] = gv
    gvb = gv.astype(BF16)

    row_i = lax.broadcasted_iota(jnp.int32, (rows, rows), 0)
    col_i = lax.broadcasted_iota(jnp.int32, (rows, rows), 1)
    tril = col_i <= row_i
    gw = d // GMLP_GROUPS
    ws = [jnp.where(tril, ws_ref[g], 0.0).astype(BF16) for g in range(GMLP_GROUPS)]
    s_chunks = []
    for c in range(tm // rows):
        r0 = c * rows
        mix = jnp.concatenate(
            [jnp.dot(ws[g], gvb[r0:r0 + rows, g * gw:(g + 1) * gw], preferred_element_type=F32)
             for g in range(GMLP_GROUPS)], axis=1)
        s_chunks.append(u[r0:r0 + rows, :] * (mix + bs_ref[...]))
    s = s_chunks[0] if len(s_chunks) == 1 else jnp.concatenate(s_chunks, axis=0)
    sb = jnp.dot(s.astype(BF16), wbb_ref[...], preferred_element_type=F32)

    gate_a = jax.nn.sigmoid(
        jnp.dot(h, wgate_ref[:, 0:d], preferred_element_type=F32) + bgate_ref[0:1, :])
    gate_b = jax.nn.sigmoid(
        jnp.dot(h, wgate_ref[:, d:2 * d], preferred_element_type=F32) + bgate_ref[1:2, :])
    ga_o[0] = gate_a.astype(BF16)
    gb_o[0] = (gate_b * sb).astype(BF16)


def _proj(x, mods, norm_g, win, wgate, bgate, qg, kg, gsum, gbcast, vng, ws, bs, wbb, tm, rows,
          q_scale, want_gv):
    g, t, d = x.shape
    row_spec, mods_spec = _row_specs(x, mods, tm)
    out_shapes = [jax.ShapeDtypeStruct((g, t, d), BF16),
                  jax.ShapeDtypeStruct((g, t, d), F32),
                  jax.ShapeDtypeStruct((g, t, d), F32),
                  jax.ShapeDtypeStruct((g, t, d), BF16),
                  jax.ShapeDtypeStruct((g, t, d), BF16),
                  jax.ShapeDtypeStruct((g, t, d), BF16),
                  jax.ShapeDtypeStruct((g, t, d), BF16)]
    if want_gv:
        out_shapes.append(jax.ShapeDtypeStruct((g, t, d), F32))
    consts = (norm_g, win, wgate, bgate, qg, kg, gsum, gbcast, vng, ws, bs, wbb)
    return pl.pallas_call(
        functools.partial(_proj_kernel, rows=rows, q_scale=q_scale),
        grid=(g, t // tm),
        in_specs=[row_spec, mods_spec] + [_const_spec(a.shape) for a in consts],
        out_specs=[row_spec] * len(out_shapes),
        out_shape=out_shapes,
        compiler_params=_params(2),
        name="mixer_proj",
    )(x, mods, *consts)


def _split_maps(q):
    lane = lax.broadcasted_iota(jnp.int32, q.shape, 1)
    half = q.shape[1] // 2
    zero = jnp.zeros_like(q)
    return jnp.where(lane < half, q, zero), jnp.where(lane >= half, q, zero)


def _qk(a, k):
    return lax.dot_general(a, k, (((1,), (1,)), ((), ())), preferred_element_type=F32)


def _lambda(lq_ref, lam_init):
    lq = lq_ref[...]
    t1 = jnp.sum(lq[0:1, :] * lq[1:2, :], axis=-1, keepdims=True)
    t2 = jnp.sum(lq[2:3, :] * lq[3:4, :], axis=-1, keepdims=True)
    return jnp.exp(t1) - jnp.exp(t2) + lam_init


def _finish_heads(acc1, l1, acc2, l2, lam, sg, lam_init):
    o = acc1 / l1 - lam * (acc2 / l2)
    return (_rms(o) * sg) * (1.0 - lam_init)


def _attn_kernel(slope_ref, tile_ref, period_ref, bias_ref, first_ref, last_ref, dist_ref,
                 q_ref, k_ref, v_ref, eqt_ref, ek_ref, lq_ref, sg_ref, o_ref,
                 kx_scr, vt_scr, bias_scr, at_scr, s_scr, p_last_scr, smax_scr, m_scr, fin_scr,
                 acc_scr, *, lam_init, wide, tq, n_steps):
    h = pl.program_id(0)
    t_all = k_ref.shape[1]
    slope = slope_ref[h]
    per_wide = wide // tq

    @pl.when(pl.program_id(1) == 0)
    def _():
        for c in range(t_all // wide):
            kx_scr[c * wide:(c + 1) * wide, LANES:2 * LANES] = ek_ref[0]
        vt_scr[LANES:, :] = jnp.ones((vt_scr.shape[0] - LANES, t_all), BF16)
        s_pos = lax.broadcasted_iota(jnp.int32, (wide, tq), 0)
        for c in range(per_wide):
            t_pos = c * tq + lax.broadcasted_iota(jnp.int32, (wide, tq), 1)
            visible = (s_pos // CHUNK) <= (t_pos // CHUNK)
            fix = (jnp.abs(t_pos - s_pos) - (t_pos - s_pos)).astype(F32)
            bias_scr[c] = jnp.where(visible, -slope * fix, NEG_BIG)

    kx_scr[:, 0:LANES] = k_ref[0]
    for c in range(t_all // wide):
        vt_scr[0:LANES, c * wide:(c + 1) * wide] = (
            v_ref[0, c * wide:(c + 1) * wide, :].astype(F32).T.astype(BF16))
    p_last_scr[...] = jnp.zeros(p_last_scr.shape, BF16)
    acc_scr[...] = jnp.zeros(acc_scr.shape, F32)
    m_scr[...] = jnp.full(m_scr.shape, NEG_BIG, F32)

    def build_queries(tile):
        r0 = pl.multiple_of(tile * tq, tq)
        qt = q_ref[0, pl.ds(r0, tq), :].astype(F32).T
        row = lax.broadcasted_iota(jnp.int32, qt.shape, 0)
        zero = jnp.zeros_like(qt)
        at_scr[0, 0:LANES, :] = jnp.where(row < LANES // 2, qt, zero).astype(BF16)
        at_scr[1, 0:LANES, :] = jnp.where(row >= LANES // 2, qt, zero).astype(BF16)
        alibi_rows = eqt_ref[0, tile % per_wide]
        at_scr[0, LANES:2 * LANES, :] = alibi_rows
        at_scr[1, LANES:2 * LANES, :] = alibi_rows

    sub = p_last_scr.shape[1]
    n_sub = wide // sub

    def scores(e, slot, masked):
        kx = kx_scr[pl.ds(pl.multiple_of(period_ref[e] * wide, wide), wide), :]
        for mp in range(2):
            s = jnp.dot(kx, at_scr[mp], preferred_element_type=F32)
            if masked:
                s = s + bias_scr[bias_ref[e]]
            s_scr[slot, mp] = s
            smax_scr[slot, mp, 0:1, :] = jnp.max(s, axis=0, keepdims=True)

    def values_block(e, c):
        off = pl.multiple_of(period_ref[e] * wide + c * sub, sub)
        return vt_scr[:, pl.ds(off, sub)]

    def pipeline_step(e, slot, masked):
        shift = -slope * (dist_ref[e] * wide).astype(F32)
        is_first = first_ref[e] == 1
        offsets, alphas = [], []
        for mp in range(2):
            m_old = jnp.where(is_first, NEG_BIG, m_scr[mp, 0:1, :])
            m_new = jnp.maximum(m_old, smax_scr[slot, mp, 0:1, :] + shift)
            alphas.append(jnp.exp2(m_old - m_new))
            m_scr[mp, 0:1, :] = m_new
            offsets.append(m_new - shift)
        vt_pending = values_block(e - 1, n_sub - 1)
        pending = [jnp.dot(vt_pending, p_last_scr[mp], preferred_element_type=F32) for mp in range(2)]
        scores(e + 1, 1 - slot, masked)
        pv = [None, None]
        for c in range(n_sub):
            rows = slice(c * sub, (c + 1) * sub)
            for mp in range(2):
                p = jnp.exp2(s_scr[slot, mp, rows, :] - offsets[mp]).astype(BF16)
                if c == n_sub - 1:
                    p_last_scr[mp] = p
                else:
                    part = jnp.dot(values_block(e, c), p, preferred_element_type=F32)
                    pv[mp] = part if pv[mp] is None else pv[mp] + part
        for mp in range(2):
            done = acc_scr[mp] + pending[mp]
            fin_scr[mp] = done
            acc_scr[mp] = alphas[mp] * done + pv[mp]

    def finalize(tile):
        lam = _lambda(lq_ref, lam_init)
        a1 = fin_scr[0]
        a2 = fin_scr[1]
        inv1 = 1.0 / a1[LANES:LANES + 1]
        inv2 = lam / a2[LANES:LANES + 1]
        ot = a1[0:LANES] * inv1 - a2[0:LANES] * inv2
        ot = ot * lax.rsqrt(jnp.mean(ot * ot, axis=0, keepdims=True) + EPS)
        r0 = pl.multiple_of(tile * tq, tq)
        o_ref[0, pl.ds(r0, tq), :] = ((ot.T * sg_ref[...]) * (1.0 - lam_init)).astype(o_ref.dtype)

    @pl.when(first_ref[1] == 1)
    def _():
        build_queries(tile_ref[1])

    @pl.when(first_ref[1] == 1)
    def _():
        scores(1, 0, True)

    def iteration(k, carry):
        e = k + 1

        @pl.when(first_ref[e + 1] == 1)
        def _():
            build_queries(tile_ref[e + 1])

        next_masked = bias_ref[e + 1] < per_wide
        for slot in (0, 1):
            for masked in (False, True):
                cond = next_masked if masked else jnp.logical_not(next_masked)

                @pl.when(jnp.logical_and(k % 2 == slot, cond))
                def _():
                    pipeline_step(e, slot, masked)

        @pl.when(last_ref[e - 1] == 1)
        def _():
            finalize(tile_ref[e - 1])

        return carry

    lax.fori_loop(0, n_steps + 1, iteration, 0)


def _bf16_pieces(x, n):
    pieces, rest = [], np.asarray(x, np.float32)
    for _ in range(n):
        p = rest.astype(BF16).astype(np.float32)
        pieces.append(p)
        rest = rest - p
    return pieces


def _alibi_rows_cols(slopes, period):
    n_heads = slopes.shape[0]
    pos = np.arange(period)
    offs = ((pos // ALIBI_SPLIT) * ALIBI_SPLIT, pos % ALIBI_SPLIT)
    eqt = np.zeros((n_heads, LANES, period), np.float32)
    ek = np.zeros((n_heads, period, LANES), np.float32)
    lane = 0
    for piece in _bf16_pieces(slopes, 3):
        for off in offs:
            eqt[:, lane, :] = off[None, :]
            ek[:, :, lane] = -piece[:, None]
            eqt[:, lane + 1, :] = piece[:, None]
            ek[:, :, lane + 1] = off[None, :]
            lane += 2
    return jnp.asarray(eqt, BF16), jnp.asarray(ek, BF16)


def _attention_schedule(n_tiles, per_wide):
    tile, period, bias, first, last, dist = [], [], [], [], [], []
    for i in range(n_tiles):
        n_wide = i // per_wide
        for j in range(n_wide + 1):
            own = j == n_wide
            tile.append(i)
            period.append(j)
            bias.append(i % per_wide if own else per_wide)
            first.append(int(j == 0))
            last.append(int(own))
            dist.append(n_wide - j)
    pad = lambda v, fill: np.asarray([fill] + v + [fill, fill], np.int32)
    return (pad(tile, n_tiles - 1), pad(period, 0), pad(bias, per_wide), pad(first, 1),
            pad(last, 0), pad(dist, 0)), len(tile)


def _attention(q, kb, vb, slopes, lq, sg, lam_init):
    b, t, width = q.shape
    n_heads = width // LANES
    tq = min(ATTN_TQ, t)
    wide = min(ALIBI_PERIOD, t)
    assert t % wide == 0 and wide % tq == 0
    per_wide = wide // tq
    eqt, ek = _alibi_rows_cols(slopes, wide)
    eqt = eqt.reshape(n_heads, LANES, per_wide, tq).transpose(0, 2, 1, 3)
    tables, n_steps = _attention_schedule(t // tq, per_wide)
    ones_rows = 16
    n_prefetch = 1 + len(tables)
    head_spec = pl.BlockSpec((1, t, LANES), lambda h, bi, *_: (bi, 0, h))
    grid_spec = pltpu.PrefetchScalarGridSpec(
        num_scalar_prefetch=n_prefetch,
        grid=(n_heads, b),
        in_specs=[head_spec, head_spec, head_spec,
                  pl.BlockSpec((1, per_wide, LANES, tq), lambda h, bi, *_: (h, 0, 0, 0)),
                  pl.BlockSpec((1, wide, LANES), lambda h, bi, *_: (h, 0, 0)),
                  pl.BlockSpec(lq.shape, lambda h, bi, *_: (0, 0)),
                  pl.BlockSpec(sg.shape, lambda h, bi, *_: (0, 0))],
        out_specs=head_spec,
        scratch_shapes=[pltpu.VMEM((t, 2 * LANES), BF16),
                        pltpu.VMEM((LANES + ones_rows, t), BF16),
                        pltpu.VMEM((per_wide, wide, tq), F32),
                        pltpu.VMEM((2, 2 * LANES, tq), BF16),
                        pltpu.VMEM((2, 2, wide, tq), F32),
                        pltpu.VMEM((2, ATTN_SUB, tq), BF16),
                        pltpu.VMEM((2, 2, 8, tq), F32),
                        pltpu.VMEM((2, 8, tq), F32),
                        pltpu.VMEM((2, LANES + ones_rows, tq), F32),
                        pltpu.VMEM((2, LANES + ones_rows, tq), F32)],
    )
    return pl.pallas_call(
        functools.partial(_attn_kernel, lam_init=lam_init, wide=wide, tq=tq, n_steps=n_steps),
        grid_spec=grid_spec,
        out_shape=jax.ShapeDtypeStruct((b, t, width), BF16),
        compiler_params=_params(2),
        name="diff_attention",
    )(jnp.asarray(slopes, F32), *tables, q, kb, vb, eqt, ek, lq, sg)


def _attn_cache_kernel(q_ref, kn_ref, vn_ref, ck_ref, cv_ref, lq_ref, sg_ref, o_ref, *,
                       lam_init, slopes):
    tn = q_ref.shape[1]
    past = ck_ref.shape[1]
    lam = _lambda(lq_ref, lam_init)

    def distance(k0, width):
        qpos = past + lax.broadcasted_iota(jnp.int32, (tn, width), 0)
        kpos = k0 + lax.broadcasted_iota(jnp.int32, (tn, width), 1)
        visible = (kpos // CHUNK) <= (qpos // CHUNK)
        return visible, jnp.abs(qpos - kpos).astype(F32)

    vis_c, dist_c = distance(0, past)
    vis_n, dist_n = distance(past, tn)

    for h, slope in enumerate(slopes):
        cols = slice(h * LANES, (h + 1) * LANES)
        q1, q2 = _split_maps(q_ref[0, :, cols])
        ck = ck_ref[0, :, cols].astype(BF16)
        cv = cv_ref[0, :, cols].astype(BF16)
        kn = kn_ref[0, :, cols]
        vn = vn_ref[0, :, cols]
        bias_c = jnp.where(vis_c, -slope * dist_c, NEG_BIG)
        bias_n = jnp.where(vis_n, -slope * dist_n, NEG_BIG)

        def one_map(a):
            sc = _qk(a, ck) + bias_c
            sn = _qk(a, kn) + bias_n
            m = jnp.maximum(jnp.max(sc, axis=-1, keepdims=True), jnp.max(sn, axis=-1, keepdims=True))
            pc = jnp.exp2(sc - m)
            pn = jnp.exp2(sn - m)
            l = jnp.sum(pc, axis=-1, keepdims=True) + jnp.sum(pn, axis=-1, keepdims=True)
            acc = (jnp.dot(pc.astype(BF16), cv, preferred_element_type=F32)
                   + jnp.dot(pn.astype(BF16), vn, preferred_element_type=F32))
            return l, acc

        l1, acc1 = one_map(q1)
        l2, acc2 = one_map(q2)
        o_ref[0, :, cols] = _finish_heads(acc1, l1, acc2, l2, lam, sg_ref[...],
                                          lam_init).astype(o_ref.dtype)


def _attention_cached(q, kb, vb, cache_k, cache_v, slopes, lq, sg, lam_init):
    b, tn, width = q.shape
    past = cache_k.shape[1]
    new_spec = pl.BlockSpec((1, tn, width), lambda bi: (bi, 0, 0))
    past_spec = pl.BlockSpec((1, past, width), lambda bi: (bi, 0, 0))
    return pl.pallas_call(
        functools.partial(_attn_cache_kernel, lam_init=lam_init,
                          slopes=tuple(float(s) for s in slopes)),
        grid=(b,),
        in_specs=[new_spec, new_spec, new_spec, past_spec, past_spec,
                  pl.BlockSpec(lq.shape, lambda bi: (0, 0)),
                  pl.BlockSpec(sg.shape, lambda bi: (0, 0))],
        out_specs=new_spec,
        out_shape=jax.ShapeDtypeStruct((b, tn, width), BF16),
        compiler_params=_params(1),
        name="diff_attention_cached",
    )(q, kb, vb, cache_k, cache_v, lq, sg)


def _out_kernel(x_ref, o_ref, ga_ref, gb_ref, mods_ref, ng_ref, wba_ref, wout_ref, wgu_ref, wd_ref,
                y_ref):
    x = x_ref[0]
    oa = jnp.dot(o_ref[0], wba_ref[...], preferred_element_type=F32)
    mixed = (ga_ref[0].astype(F32) * oa + gb_ref[0].astype(F32)).astype(BF16)
    x = x + mods_ref[0, 5] * jnp.dot(mixed, wout_ref[...], preferred_element_type=F32)
    h = _mod_norm(x, ng_ref[2:3, :], mods_ref, 6).astype(BF16)
    x = x + 0.5 * mods_ref[0, 8] * _swiglu(h, wgu_ref, wd_ref)
    y_ref[0] = _rms(x) * ng_ref[3:4, :]


def _mix_ffn2(x, o, ga, gb, mods, norm_g, wba, wout, wgu, wd, tm):
    g, t, d = x.shape
    row_spec, mods_spec = _row_specs(x, mods, tm)
    consts = (norm_g, wba, wout, wgu, wd)
    return pl.pallas_call(
        _out_kernel,
        grid=(g, t // tm),
        in_specs=[row_spec] * 4 + [mods_spec] + [_const_spec(a.shape) for a in consts],
        out_specs=row_spec,
        out_shape=jax.ShapeDtypeStruct((g, t, d), F32),
        compiler_params=_params(2),
        name="mix_ffn2",
    )(x, o, ga, gb, mods, *consts)


def _row_tile(t, target):
    tm = min(t, target)
    while t % tm:
        tm //= 2
    return tm


def kernel(x_prompt, x_sample, cache_k, cache_v, c_prompt, c_sample, ada_w, ada_b, norm_g, ffn1_wgu,
           ffn1_wd, w_in, q_norm_g, k_norm_g, lambda_qk, attn_subln_g, gmlp_vnorm_g, gmlp_ws, gmlp_bs,
           w_gate, b_gate, w_branch, w_out, ffn2_wgu, ffn2_wd):
    assert ada_w.shape[0] == 1, "single-layer step"
    bp, tp, d = x_prompt.shape
    bs_, ts, _ = x_sample.shape
    _, _, past, n_heads, _, hd = cache_k.shape
    assert n_heads == 8 and 2 * hd == LANES and n_heads * LANES == d
    lam_init = _lambda_init(0)
    log2e = math.log2(math.e)
    slopes = (2.0 ** (-8.0 * np.arange(1, n_heads + 1) / n_heads) * log2e).astype(np.float32)
    q_scale = float(hd) ** -0.5 * log2e

    assert ffn1_wd.shape[1] % FFN_CHUNK == 0 and w_in.shape[2] == 5 * d
    wgu1, wd1 = ffn1_wgu[0].astype(BF16), ffn1_wd[0].astype(BF16)
    wgu2, wd2 = ffn2_wgu[0].astype(BF16), ffn2_wd[0].astype(BF16)
    win = w_in[0].astype(BF16)
    wgate = w_gate[0].astype(BF16)
    bgate = b_gate[0].reshape(2, d)
    wba = w_branch[0, :d].astype(BF16)
    wbb = w_branch[0, d:].astype(BF16)
    wout = w_out[0].astype(BF16)
    ng = norm_g[0]
    qg = jnp.tile(q_norm_g[0], d // hd).reshape(1, d)
    kg = jnp.tile(k_norm_g[0], d // hd).reshape(1, d)
    n_groups = d // hd
    member = (np.arange(d)[:, None] // hd == np.arange(LANES)[None, :]).astype(np.float32)
    gsum = jnp.asarray(member / hd, BF16)
    gbcast = jnp.asarray(np.concatenate([member.T, member.T], axis=0), BF16)
    assert n_groups <= LANES
    vng = gmlp_vnorm_g[0].reshape(1, d)
    sg = attn_subln_g[0].reshape(1, LANES)
    lq = lambda_qk[0]

    mods = _mods(jnp.concatenate([c_prompt, c_sample], axis=0), ada_w[0], ada_b[0])
    mods = mods.reshape(bp + bs_, N_MODS, d)
    mods_p = mods[:bp].reshape(bp, N_MODS, 1, d)
    mods_s = jnp.repeat(mods[bp:].transpose(1, 0, 2), ts, axis=1).reshape(1, N_MODS, bs_ * ts, d)

    def gmlp_consts(t):
        rows = min(t, GMLP_CHUNK)
        ws = gmlp_ws[0][:, :rows, :rows]
        bs = jnp.repeat(gmlp_bs[0][:, :rows].T, d // GMLP_GROUPS, axis=1)
        return rows, ws, bs

    def layer(x, mods_x, seq, attend, tm_ffn, tm_proj, want_gv):
        _, t, _ = x.shape
        rows, ws, bs = gmlp_consts(seq)
        x1 = _ffn1(x, mods_x, ng, wgu1, wd1, _row_tile(t, tm_ffn))
        tmp = max(_row_tile(t, tm_proj), rows)
        outs = _proj(x1, mods_x, ng, win, wgate, bgate, qg, kg, gsum, gbcast, vng, ws, bs, wbb,
                     tmp, rows, q_scale, want_gv)
        q, k, v, kb, vb, ga, gb = outs[:7]
        o = attend(q, kb, vb)
        y = _mix_ffn2(x1, o, ga, gb, mods_x, ng, wba, wout, wgu2, wd2, _row_tile(t, tm_ffn))
        return y, k, v, (outs[7] if want_gv else None)

    y_p, k_p, v_p, _ = layer(
        x_prompt, mods_p, tp, lambda q, kb, vb: _attention(q, kb, vb, slopes, lq, sg, lam_init),
        512, 512, False)

    ck = cache_k[0].reshape(bs_, past, d)
    cv = cache_v[0].reshape(bs_, past, d)

    def attend_sample(q, kb, vb):
        shape = (bs_, ts, d)
        o = _attention_cached(q.reshape(shape), kb.reshape(shape), vb.reshape(shape), ck, cv,
                              slopes, lq, sg, lam_init)
        return o.reshape(1, bs_ * ts, d)

    y_s, k_s, v_s, gv_s = layer(
        x_sample.reshape(1, bs_ * ts, d), mods_s, ts, attend_sample, 512, 512, True)

    return (y_p, y_s.reshape(bs_, ts, d),
            k_p.reshape(1, bp, tp, n_heads, 2, hd), v_p.reshape(1, bp, tp, n_heads, 2 * hd),
            k_s.reshape(1, bs_, ts, n_heads, 2, hd), v_s.reshape(1, bs_, ts, n_heads, 2 * hd),
            gv_s.reshape(1, bs_, ts, d))
```

```python
import functools
import math

import numpy as np
import jax
import jax.numpy as jnp
from jax import lax
from jax.experimental import pallas as pl
from jax.experimental.pallas import tpu as pltpu

F32 = jnp.float32
BF16 = jnp.bfloat16

EPS = 1e-6
CHUNK = 64
GMLP_CHUNK = 128
GMLP_GROUPS = 8
N_MODS = 9
NEG_BIG = -1e30

V7X_VMEM_BYTES = 64 * 1024 * 1024
VMEM_LIMIT_BYTES = V7X_VMEM_BYTES - 8 * 1024 * 1024
LANES = 128
SUBLANES = 8
BF16_SUBLANES = 16
MXU_DIM = 256
ATTN_SUB = 2 * MXU_DIM
ATTN_TQ = 1024
FFN_CHUNK = 256
ALIBI_SPLIT = 256
ALIBI_PERIOD = 1024


def _lambda_init(layer_idx):
    return 0.8 - 0.6 * math.exp(-0.3 * layer_idx)


def _const_spec(shape):
    nd = len(shape)
    return pl.BlockSpec(shape, lambda *_: (0,) * nd, pipeline_mode=pl.Buffered(1))


def _params(n_axes):
    return pltpu.CompilerParams(
        dimension_semantics=("arbitrary",) * n_axes, vmem_limit_bytes=VMEM_LIMIT_BYTES)


def _mods_kernel(c_ref, w_ref, b_ref, o_ref):
    c = c_ref[...]
    h = c * jax.nn.sigmoid(c)
    o_ref[...] = jnp.dot(h, w_ref[...], preferred_element_type=F32,
                         precision=lax.Precision.HIGHEST) + b_ref[...]


def _mods(c_all, ada_w, ada_b):
    n, d = c_all.shape
    width = ada_w.shape[1]
    tn = d
    return pl.pallas_call(
        _mods_kernel,
        grid=(width // tn,),
        in_specs=[pl.BlockSpec((n, d), lambda j: (0, 0)),
                  pl.BlockSpec((d, tn), lambda j: (0, j)),
                  pl.BlockSpec((1, tn), lambda j: (0, j))],
        out_specs=pl.BlockSpec((n, tn), lambda j: (0, j)),
        out_shape=jax.ShapeDtypeStruct((n, width), F32),
        compiler_params=_params(1),
        name="adaln_mods",
    )(c_all, ada_w, ada_b.reshape(1, width))


def _rms(x):
    return x * lax.rsqrt(jnp.mean(x * x, axis=-1, keepdims=True) + EPS)


def _mod_norm(x, g, mods_ref, first):
    return (_rms(x) * g) * (1.0 + mods_ref[0, first + 1]) + mods_ref[0, first]


def _swiglu(h, wgu_ref, wd_ref):
    f = wd_ref.shape[0]
    acc = None
    for c0 in range(0, f, FFN_CHUNK):
        gate = jnp.dot(h, wgu_ref[:, c0:c0 + FFN_CHUNK], preferred_element_type=F32)
        up = jnp.dot(h, wgu_ref[:, f + c0:f + c0 + FFN_CHUNK], preferred_element_type=F32)
        a = (gate * jax.nn.sigmoid(gate) * up).astype(BF16)
        part = jnp.dot(a, wd_ref[c0:c0 + FFN_CHUNK, :], preferred_element_type=F32)
        acc = part if acc is None else acc + part
    return acc


def _gelu_tanh(x):
    return 0.5 * x * (1.0 + jnp.tanh(math.sqrt(2.0 / math.pi) * (x + 0.044715 * (x * x * x))))


def _row_specs(x, mods, tm):
    _, _, d = x.shape
    r = mods.shape[2]
    row_spec = pl.BlockSpec((1, tm, d), lambda g, i: (g, i, 0))
    if r == 1:
        mods_spec = pl.BlockSpec((1, N_MODS, 1, d), lambda g, i: (g, 0, 0, 0))
    else:
        mods_spec = pl.BlockSpec((1, N_MODS, tm, d), lambda g, i: (g, 0, i, 0))
    return row_spec, mods_spec


def _ffn_kernel(x_ref, mods_ref, ng_ref, wgu_ref, wd_ref, o_ref):
    x = x_ref[0]
    h = _mod_norm(x, ng_ref[0:1, :], mods_ref, 0).astype(BF16)
    o_ref[0] = x + 0.5 * mods_ref[0, 2] * _swiglu(h, wgu_ref, wd_ref)


def _ffn1(x, mods, norm_g, wgu, wd, tm):
    g, t, d = x.shape
    row_spec, mods_spec = _row_specs(x, mods, tm)
    return pl.pallas_call(
        _ffn_kernel,
        grid=(g, t // tm),
        in_specs=[row_spec, mods_spec,
                  _const_spec(norm_g.shape), _const_spec(wgu.shape), _const_spec(wd.shape)],
        out_specs=row_spec,
        out_shape=jax.ShapeDtypeStruct((g, t, d), F32),
        compiler_params=_params(2),
        name="ffn1",
    )(x, mods, norm_g, wgu, wd)


def _proj_kernel(x_ref, mods_ref, ng_ref, win_ref, wgate_ref, bgate_ref, qg_ref, kg_ref, gsum_ref,
                 gbcast_ref, vng_ref, ws_ref, bs_ref, wbb_ref,
                 q_o, k_o, v_o, kb_o, vb_o, ga_o, gb_o, *maybe_gv_o, rows, q_scale):
    x = x_ref[0]
    tm, d = x.shape
    h = _mod_norm(x, ng_ref[1:2, :], mods_ref, 3).astype(BF16)

    def w_in(j):
        return win_ref[:, j * d:(j + 1) * d]

    def group_norm(z, g):
        ms = jnp.dot((z * z).astype(BF16), gsum_ref[...], preferred_element_type=F32)
        r = lax.rsqrt(ms + EPS)
        r_hi = r.astype(BF16)
        r_lo = (r - r_hi.astype(F32)).astype(BF16)
        scale = jnp.dot(jnp.concatenate([r_hi, r_lo], axis=1), gbcast_ref[...],
                        preferred_element_type=F32)
        return z * scale * g

    q = group_norm(jnp.dot(h, w_in(0), preferred_element_type=F32), qg_ref[...])
    q_o[0] = (q * q_scale).astype(BF16)
    k = group_norm(jnp.dot(h, w_in(1), preferred_element_type=F32), kg_ref[...])
    k_o[0] = k
    kb_o[0] = k.astype(BF16)
    v = jnp.dot(h, w_in(2), preferred_element_type=F32)
    v_o[0] = v
    vb_o[0] = v.astype(BF16)

    u = _gelu_tanh(jnp.dot(h, w_in(3), preferred_element_type=F32))
    gz = _gelu_tanh(jnp.dot(h, w_in(4), preferred_element_type=F32))
    gv = _rms(gz) * vng_ref[...]
    if maybe_gv_o:
        maybe_gv_o[0][0] = gv
    gvb = gv.astype(BF16)

    row_i = lax.broadcasted_iota(jnp.int32, (rows, rows), 0)
    col_i = lax.broadcasted_iota(jnp.int32, (rows, rows), 1)
    tril = col_i <= row_i
    gw = d // GMLP_GROUPS
    ws = [jnp.where(tril, ws_ref[g], 0.0).astype(BF16) for g in range(GMLP_GROUPS)]
    s_chunks = []
    for c in range(tm // rows):
        r0 = c * rows
        mix = jnp.concatenate(
            [jnp.dot(ws[g], gvb[r0:r0 + rows, g * gw:(g + 1) * gw], preferred_element_type=F32)
             for g in range(GMLP_GROUPS)], axis=1)
        s_chunks.append(u[r0:r0 + rows, :] * (mix + bs_ref[...]))
    s = s_chunks[0] if len(s_chunks) == 1 else jnp.concatenate(s_chunks, axis=0)
    sb = jnp.dot(s.astype(BF16), wbb_ref[...], preferred_element_type=F32)

    gate_a = jax.nn.sigmoid(
        jnp.dot(h, wgate_ref[:, 0:d], preferred_element_type=F32) + bgate_ref[0:1, :])
    gate_b = jax.nn.sigmoid(
        jnp.dot(h, wgate_ref[:, d:2 * d], preferred_element_type=F32) + bgate_ref[1:2, :])
    ga_o[0] = gate_a.astype(BF16)
    gb_o[0] = (gate_b * sb).astype(BF16)


def _proj(x, mods, norm_g, win, wgate, bgate, qg, kg, gsum, gbcast, vng, ws, bs, wbb, tm, rows,
          q_scale, want_gv):
    g, t, d = x.shape
    row_spec, mods_spec = _row_specs(x, mods, tm)
    out_shapes = [jax.ShapeDtypeStruct((g, t, d), BF16),
                  jax.ShapeDtypeStruct((g, t, d), F32),
                  jax.ShapeDtypeStruct((g, t, d), F32),
                  jax.ShapeDtypeStruct((g, t, d), BF16),
                  jax.ShapeDtypeStruct((g, t, d), BF16),
                  jax.ShapeDtypeStruct((g, t, d), BF16),
                  jax.ShapeDtypeStruct((g, t, d), BF16)]
    if want_gv:
        out_shapes.append(jax.ShapeDtypeStruct((g, t, d), F32))
    consts = (norm_g, win, wgate, bgate, qg, kg, gsum, gbcast, vng, ws, bs, wbb)
    return pl.pallas_call(
        functools.partial(_proj_kernel, rows=rows, q_scale=q_scale),
        grid=(g, t // tm),
        in_specs=[row_spec, mods_spec] + [_const_spec(a.shape) for a in consts],
        out_specs=[row_spec] * len(out_shapes),
        out_shape=out_shapes,
        compiler_params=_params(2),
        name="mixer_proj",
    )(x, mods, *consts)


def _split_maps(q):
    lane = lax.broadcasted_iota(jnp.int32, q.shape, 1)
    half = q.shape[1] // 2
    zero = jnp.zeros_like(q)
    return jnp.where(lane < half, q, zero), jnp.where(lane >= half, q, zero)


def _qk(a, k):
    return lax.dot_general(a, k, (((1,), (1,)), ((), ())), preferred_element_type=F32)


def _lambda(lq_ref, lam_init):
    lq = lq_ref[...]
    t1 = jnp.sum(lq[0:1, :] * lq[1:2, :], axis=-1, keepdims=True)
    t2 = jnp.sum(lq[2:3, :] * lq[3:4, :], axis=-1, keepdims=True)
    return jnp.exp(t1) - jnp.exp(t2) + lam_init


def _finish_heads(acc1, l1, acc2, l2, lam, sg, lam_init):
    o = acc1 / l1 - lam * (acc2 / l2)
    return (_rms(o) * sg) * (1.0 - lam_init)


def _attn_kernel(slope_ref, tile_ref, period_ref, bias_ref, first_ref, last_ref, dist_ref,
                 q_ref, k_ref, v_ref, eqt_ref, ek_ref, lq_ref, sg_ref, o_ref,
                 kx_scr, vt_scr, bias_scr, at_scr, s_scr, p_last_scr, smax_scr, m_scr, fin_scr,
                 acc_scr, *, lam_init, wide, tq, n_steps):
    h = pl.program_id(0)
    t_all = k_ref.shape[1]
    slope = slope_ref[h]
    per_wide = wide // tq

    @pl.when(pl.program_id(1) == 0)
    def _():
        for c in range(t_all // wide):
            kx_scr[c * wide:(c + 1) * wide, LANES:2 * LANES] = ek_ref[0]
        vt_scr[LANES:, :] = jnp.ones((vt_scr.shape[0] - LANES, t_all), BF16)
        s_pos = lax.broadcasted_iota(jnp.int32, (wide, tq), 0)
        for c in range(per_wide):
            t_pos = c * tq + lax.broadcasted_iota(jnp.int32, (wide, tq), 1)
            visible = (s_pos // CHUNK) <= (t_pos // CHUNK)
            fix = (jnp.abs(t_pos - s_pos) - (t_pos - s_pos)).astype(F32)
            bias_scr[c] = jnp.where(visible, -slope * fix, NEG_BIG)

    kx_scr[:, 0:LANES] = k_ref[0]
    for c in range(t_all // wide):
        vt_scr[0:LANES, c * wide:(c + 1) * wide] = (
            v_ref[0, c * wide:(c + 1) * wide, :].astype(F32).T.astype(BF16))
    p_last_scr[...] = jnp.zeros(p_last_scr.shape, BF16)
    acc_scr[...] = jnp.zeros(acc_scr.shape, F32)
    m_scr[...] = jnp.full(m_scr.shape, NEG_BIG, F32)

    def build_queries(tile):
        r0 = pl.multiple_of(tile * tq, tq)
        qt = q_ref[0, pl.ds(r0, tq), :].astype(F32).T
        row = lax.broadcasted_iota(jnp.int32, qt.shape, 0)
        zero = jnp.zeros_like(qt)
        at_scr[0, 0:LANES, :] = jnp.where(row < LANES // 2, qt, zero).astype(BF16)
        at_scr[1, 0:LANES, :] = jnp.where(row >= LANES // 2, qt, zero).astype(BF16)
        alibi_rows = eqt_ref[0, tile % per_wide]
        at_scr[0, LANES:2 * LANES, :] = alibi_rows
        at_scr[1, LANES:2 * LANES, :] = alibi_rows

    sub = p_last_scr.shape[1]
    n_sub = wide // sub

    def scores(e, slot, masked):
        kx = kx_scr[pl.ds(pl.multiple_of(period_ref[e] * wide, wide), wide), :]
        for mp in range(2):
            s = jnp.dot(kx, at_scr[mp], preferred_element_type=F32)
            if masked:
                s = s + bias_scr[bias_ref[e]]
            s_scr[slot, mp] = s
            smax_scr[slot, mp, 0:1, :] = jnp.max(s, axis=0, keepdims=True)

    def values_block(e, c):
        off = pl.multiple_of(period_ref[e] * wide + c * sub, sub)
        return vt_scr[:, pl.ds(off, sub)]

    def pipeline_step(e, slot, masked, final=False):
        shift = -slope * (dist_ref[e] * wide).astype(F32)
        is_first = first_ref[e] == 1
        offsets, alphas = [], []
        for mp in range(2):
            m_old = jnp.where(is_first, NEG_BIG, m_scr[mp, 0:1, :])
            m_new = jnp.maximum(m_old, smax_scr[slot, mp, 0:1, :] + shift)
            alphas.append(jnp.exp2(m_old - m_new))
            m_scr[mp, 0:1, :] = m_new
            offsets.append(m_new - shift)
        vt_pending = values_block(e - 1, n_sub - 1)
        pending = [jnp.dot(vt_pending, p_last_scr[mp], preferred_element_type=F32) for mp in range(2)]
        if not final:
            scores(e + 1, 1 - slot, masked)
        pv = [None, None]
        for c in range(n_sub):
            rows = slice(c * sub, (c + 1) * sub)
            for mp in range(2):
                p = jnp.exp2(s_scr[slot, mp, rows, :] - offsets[mp]).astype(BF16)
                if c == n_sub - 1 and not final:
                    p_last_scr[mp] = p
                else:
                    part = jnp.dot(values_block(e, c), p, preferred_element_type=F32)
                    pv[mp] = part if pv[mp] is None else pv[mp] + part
        for mp in range(2):
            done = acc_scr[mp] + pending[mp]
            fin_scr[mp] = done
            acc_scr[mp] = alphas[mp] * done + pv[mp]

    def finalize(tile, src=fin_scr):
        lam = _lambda(lq_ref, lam_init)
        a1 = src[0]
        a2 = src[1]
        inv1 = 1.0 / a1[LANES:LANES + 1]
        inv2 = lam / a2[LANES:LANES + 1]
        ot = a1[0:LANES] * inv1 - a2[0:LANES] * inv2
        ot = ot * lax.rsqrt(jnp.mean(ot * ot, axis=0, keepdims=True) + EPS)
        r0 = pl.multiple_of(tile * tq, tq)
        o_ref[0, pl.ds(r0, tq), :] = ((ot.T * sg_ref[...]) * (1.0 - lam_init)).astype(o_ref.dtype)

    @pl.when(first_ref[1] == 1)
    def _():
        build_queries(tile_ref[1])

    @pl.when(first_ref[1] == 1)
    def _():
        scores(1, 0, True)

    def iteration(k, carry):
        e = k + 1

        @pl.when(first_ref[e + 1] == 1)
        def _():
            build_queries(tile_ref[e + 1])

        next_masked = bias_ref[e + 1] < per_wide
        for slot in (0, 1):
            for masked in (False, True):
                cond = next_masked if masked else jnp.logical_not(next_masked)

                @pl.when(jnp.logical_and(k % 2 == slot, cond))
                def _():
                    pipeline_step(e, slot, masked)

        @pl.when(last_ref[e - 1] == 1)
        def _():
            finalize(tile_ref[e - 1])

        return carry

    lax.fori_loop(0, n_steps - 1, iteration, 0)

    e_last = n_steps

    @pl.when(first_ref[1] == 1)
    def _():
        pipeline_step(e_last, (n_steps - 1) % 2, False, final=True)

    @pl.when(last_ref[e_last - 1] == 1)
    def _():
        finalize(tile_ref[e_last - 1])

    @pl.when(first_ref[1] == 1)
    def _():
        finalize(tile_ref[e_last], acc_scr)


def _bf16_pieces(x, n):
    pieces, rest = [], np.asarray(x, np.float32)
    for _ in range(n):
        p = rest.astype(BF16).astype(np.float32)
        pieces.append(p)
        rest = rest - p
    return pieces


def _alibi_rows_cols(slopes, period):
    n_heads = slopes.shape[0]
    pos = np.arange(period)
    offs = ((pos // ALIBI_SPLIT) * ALIBI_SPLIT, pos % ALIBI_SPLIT)
    eqt = np.zeros((n_heads, LANES, period), np.float32)
    ek = np.zeros((n_heads, period, LANES), np.float32)
    lane = 0
    for piece in _bf16_pieces(slopes, 3):
        for off in offs:
            eqt[:, lane, :] = off[None, :]
            ek[:, :, lane] = -piece[:, None]
            eqt[:, lane + 1, :] = piece[:, None]
            ek[:, :, lane + 1] = off[None, :]
            lane += 2
    return jnp.asarray(eqt, BF16), jnp.asarray(ek, BF16)


def _attention_schedule(n_tiles, per_wide):
    tile, period, bias, first, last, dist = [], [], [], [], [], []
    for i in range(n_tiles):
        n_wide = i // per_wide
        for j in range(n_wide + 1):
            own = j == n_wide
            tile.append(i)
            period.append(j)
            bias.append(i % per_wide if own else per_wide)
            first.append(int(j == 0))
            last.append(int(own))
            dist.append(n_wide - j)
    pad = lambda v, fill: np.asarray([fill] + v + [fill, fill], np.int32)
    return (pad(tile, n_tiles - 1), pad(period, 0), pad(bias, per_wide), pad(first, 1),
            pad(last, 0), pad(dist, 0)), len(tile)


def _attention(q, kb, vb, slopes, lq, sg, lam_init):
    b, t, width = q.shape
    n_heads = width // LANES
    tq = min(ATTN_TQ, t)
    wide = min(ALIBI_PERIOD, t)
    assert t % wide == 0 and wide % tq == 0
    per_wide = wide // tq
    eqt, ek = _alibi_rows_cols(slopes, wide)
    eqt = eqt.reshape(n_heads, LANES, per_wide, tq).transpose(0, 2, 1, 3)
    tables, n_steps = _attention_schedule(t // tq, per_wide)
    ones_rows = BF16_SUBLANES
    n_prefetch = 1 + len(tables)
    head_spec = pl.BlockSpec((1, t, LANES), lambda h, bi, *_: (bi, 0, h))
    grid_spec = pltpu.PrefetchScalarGridSpec(
        num_scalar_prefetch=n_prefetch,
        grid=(n_heads, b),
        in_specs=[head_spec, head_spec, head_spec,
                  pl.BlockSpec((1, per_wide, LANES, tq), lambda h, bi, *_: (h, 0, 0, 0)),
                  pl.BlockSpec((1, wide, LANES), lambda h, bi, *_: (h, 0, 0)),
                  pl.BlockSpec(lq.shape, lambda h, bi, *_: (0, 0)),
                  pl.BlockSpec(sg.shape, lambda h, bi, *_: (0, 0))],
        out_specs=head_spec,
        scratch_shapes=[pltpu.VMEM((t, 2 * LANES), BF16),
                        pltpu.VMEM((LANES + ones_rows, t), BF16),
                        pltpu.VMEM((per_wide, wide, tq), F32),
                        pltpu.VMEM((2, 2 * LANES, tq), BF16),
                        pltpu.VMEM((2, 2, wide, tq), F32),
                        pltpu.VMEM((2, ATTN_SUB, tq), BF16),
                        pltpu.VMEM((2, 2, SUBLANES, tq), F32),
                        pltpu.VMEM((2, SUBLANES, tq), F32),
                        pltpu.VMEM((2, LANES + ones_rows, tq), F32),
                        pltpu.VMEM((2, LANES + ones_rows, tq), F32)],
    )
    return pl.pallas_call(
        functools.partial(_attn_kernel, lam_init=lam_init, wide=wide, tq=tq, n_steps=n_steps),
        grid_spec=grid_spec,
        out_shape=jax.ShapeDtypeStruct((b, t, width), BF16),
        compiler_params=_params(2),
        name="diff_attention",
    )(jnp.asarray(slopes, F32), *tables, q, kb, vb, eqt, ek, lq, sg)


def _attn_cache_kernel(q_ref, kn_ref, vn_ref, ck_ref, cv_ref, lq_ref, sg_ref, o_ref, *,
                       lam_init, slopes):
    tn = q_ref.shape[1]
    past = ck_ref.shape[1]
    lam = _lambda(lq_ref, lam_init)

    def distance(k0, width):
        qpos = past + lax.broadcasted_iota(jnp.int32, (tn, width), 0)
        kpos = k0 + lax.broadcasted_iota(jnp.int32, (tn, width), 1)
        visible = (kpos // CHUNK) <= (qpos // CHUNK)
        return visible, jnp.abs(qpos - kpos).astype(F32)

    vis_c, dist_c = distance(0, past)
    vis_n, dist_n = distance(past, tn)

    for h, slope in enumerate(slopes):
        cols = slice(h * LANES, (h + 1) * LANES)
        q1, q2 = _split_maps(q_ref[0, :, cols])
        ck = ck_ref[0, :, cols].astype(BF16)
        cv = cv_ref[0, :, cols].astype(BF16)
        kn = kn_ref[0, :, cols]
        vn = vn_ref[0, :, cols]
        bias_c = jnp.where(vis_c, -slope * dist_c, NEG_BIG)
        bias_n = jnp.where(vis_n, -slope * dist_n, NEG_BIG)

        def one_map(a):
            sc = _qk(a, ck) + bias_c
            sn = _qk(a, kn) + bias_n
            m = jnp.maximum(jnp.max(sc, axis=-1, keepdims=True), jnp.max(sn, axis=-1, keepdims=True))
            pc = jnp.exp2(sc - m)
            pn = jnp.exp2(sn - m)
            l = jnp.sum(pc, axis=-1, keepdims=True) + jnp.sum(pn, axis=-1, keepdims=True)
            acc = (jnp.dot(pc.astype(BF16), cv, preferred_element_type=F32)
                   + jnp.dot(pn.astype(BF16), vn, preferred_element_type=F32))
            return l, acc

        l1, acc1 = one_map(q1)
        l2, acc2 = one_map(q2)
        o_ref[0, :, cols] = _finish_heads(acc1, l1, acc2, l2, lam, sg_ref[...],
                                          lam_init).astype(o_ref.dtype)


def _attention_cached(q, kb, vb, cache_k, cache_v, slopes, lq, sg, lam_init):
    b, tn, width = q.shape
    past = cache_k.shape[1]
    new_spec = pl.BlockSpec((1, tn, width), lambda bi: (bi, 0, 0))
    past_spec = pl.BlockSpec((1, past, width), lambda bi: (bi, 0, 0))
    return pl.pallas_call(
        functools.partial(_attn_cache_kernel, lam_init=lam_init,
                          slopes=tuple(float(s) for s in slopes)),
        grid=(b,),
        in_specs=[new_spec, new_spec, new_spec, past_spec, past_spec,
                  pl.BlockSpec(lq.shape, lambda bi: (0, 0)),
                  pl.BlockSpec(sg.shape, lambda bi: (0, 0))],
        out_specs=new_spec,
        out_shape=jax.ShapeDtypeStruct((b, tn, width), BF16),
        compiler_params=_params(1),
        name="diff_attention_cached",
    )(q, kb, vb, cache_k, cache_v, lq, sg)


def _out_kernel(x_ref, o_ref, ga_ref, gb_ref, mods_ref, ng_ref, wba_ref, wout_ref, wgu_ref, wd_ref,
                y_ref):
    x = x_ref[0]
    oa = jnp.dot(o_ref[0], wba_ref[...], preferred_element_type=F32)
    mixed = (ga_ref[0].astype(F32) * oa + gb_ref[0].astype(F32)).astype(BF16)
    x = x + mods_ref[0, 5] * jnp.dot(mixed, wout_ref[...], preferred_element_type=F32)
    h = _mod_norm(x, ng_ref[2:3, :], mods_ref, 6).astype(BF16)
    x = x + 0.5 * mods_ref[0, 8] * _swiglu(h, wgu_ref, wd_ref)
    y_ref[0] = _rms(x) * ng_ref[3:4, :]


def _mix_ffn2(x, o, ga, gb, mods, norm_g, wba, wout, wgu, wd, tm):
    g, t, d = x.shape
    row_spec, mods_spec = _row_specs(x, mods, tm)
    consts = (norm_g, wba, wout, wgu, wd)
    return pl.pallas_call(
        _out_kernel,
        grid=(g, t // tm),
        in_specs=[row_spec] * 4 + [mods_spec] + [_const_spec(a.shape) for a in consts],
        out_specs=row_spec,
        out_shape=jax.ShapeDtypeStruct((g, t, d), F32),
        compiler_params=_params(2),
        name="mix_ffn2",
    )(x, o, ga, gb, mods, *consts)


def _row_tile(t, target):
    tm = min(t, target)
    while t % tm:
        tm //= 2
    return tm


def kernel(x_prompt, x_sample, cache_k, cache_v, c_prompt, c_sample, ada_w, ada_b, norm_g, ffn1_wgu,
           ffn1_wd, w_in, q_norm_g, k_norm_g, lambda_qk, attn_subln_g, gmlp_vnorm_g, gmlp_ws, gmlp_bs,
           w_gate, b_gate, w_branch, w_out, ffn2_wgu, ffn2_wd):
    assert ada_w.shape[0] == 1, "single-layer step"
    bp, tp, d = x_prompt.shape
    bs_, ts, _ = x_sample.shape
    _, _, past, n_heads, _, hd = cache_k.shape
    assert n_heads == 8 and 2 * hd == LANES and n_heads * LANES == d
    assert gmlp_ws.shape[1:] == (GMLP_GROUPS, GMLP_CHUNK, GMLP_CHUNK)
    lam_init = _lambda_init(0)
    log2e = math.log2(math.e)
    slopes = (2.0 ** (-8.0 * np.arange(1, n_heads + 1) / n_heads) * log2e).astype(np.float32)
    q_scale = float(hd) ** -0.5 * log2e

    assert ffn1_wd.shape[1] % FFN_CHUNK == 0 and w_in.shape[2] == 5 * d
    wgu1, wd1 = ffn1_wgu[0].astype(BF16), ffn1_wd[0].astype(BF16)
    wgu2, wd2 = ffn2_wgu[0].astype(BF16), ffn2_wd[0].astype(BF16)
    win = w_in[0].astype(BF16)
    wgate = w_gate[0].astype(BF16)
    bgate = b_gate[0].reshape(2, d)
    wba = w_branch[0, :d].astype(BF16)
    wbb = w_branch[0, d:].astype(BF16)
    wout = w_out[0].astype(BF16)
    ng = norm_g[0]
    qg = jnp.tile(q_norm_g[0], d // hd).reshape(1, d)
    kg = jnp.tile(k_norm_g[0], d // hd).reshape(1, d)
    n_groups = d // hd
    member = (np.arange(d)[:, None] // hd == np.arange(LANES)[None, :]).astype(np.float32)
    gsum = jnp.asarray(member / hd, BF16)
    gbcast = jnp.asarray(np.concatenate([member.T, member.T], axis=0), BF16)
    assert n_groups <= LANES
    vng = gmlp_vnorm_g[0].reshape(1, d)
    sg = attn_subln_g[0].reshape(1, LANES)
    lq = lambda_qk[0]

    mods = _mods(jnp.concatenate([c_prompt, c_sample], axis=0), ada_w[0], ada_b[0])
    mods = mods.reshape(bp + bs_, N_MODS, d)
    mods_p = mods[:bp].reshape(bp, N_MODS, 1, d)
    mods_s = jnp.repeat(mods[bp:].transpose(1, 0, 2), ts, axis=1).reshape(1, N_MODS, bs_ * ts, d)

    def gmlp_consts(t):
        rows = min(t, GMLP_CHUNK)
        ws = gmlp_ws[0][:, :rows, :rows]
        bs = jnp.repeat(gmlp_bs[0][:, :rows].T, d // GMLP_GROUPS, axis=1)
        return rows, ws, bs

    def layer(x, mods_x, seq, attend, tm_ffn, tm_proj, want_gv):
        _, t, _ = x.shape
        rows, ws, bs = gmlp_consts(seq)
        x1 = _ffn1(x, mods_x, ng, wgu1, wd1, _row_tile(t, tm_ffn))
        tmp = max(_row_tile(t, tm_proj), rows)
        outs = _proj(x1, mods_x, ng, win, wgate, bgate, qg, kg, gsum, gbcast, vng, ws, bs, wbb,
                     tmp, rows, q_scale, want_gv)
        q, k, v, kb, vb, ga, gb = outs[:7]
        o = attend(q, kb, vb)
        y = _mix_ffn2(x1, o, ga, gb, mods_x, ng, wba, wout, wgu2, wd2, _row_tile(t, tm_ffn))
        return y, k, v, (outs[7] if want_gv else None)

    y_p, k_p, v_p, _ = layer(
        x_prompt, mods_p, tp, lambda q, kb, vb: _attention(q, kb, vb, slopes, lq, sg, lam_init),
        512, 512, False)

    ck = cache_k[0].reshape(bs_, past, d)
    cv = cache_v[0].reshape(bs_, past, d)

    def attend_sample(q, kb, vb):
        shape = (bs_, ts, d)
        o = _attention_cached(q.reshape(shape), kb.reshape(shape), vb.reshape(shape), ck, cv,
                              slopes, lq, sg, lam_init)
        return o.reshape(1, bs_ * ts, d)

    y_s, k_s, v_s, gv_s = layer(
        x_sample.reshape(1, bs_ * ts, d), mods_s, ts, attend_sample, 512, 512, True)

    return (y_p, y_s.reshape(bs_, ts, d),
            k_p.reshape(1, bp, tp, n_heads, 2, hd), v_p.reshape(1, bp, tp, n_heads, 2 * hd),
            k_s.reshape(1, bs_, ts, n_heads, 2, hd), v_s.reshape(1, bs_, ts, n_heads, 2 * hd),
            gv_s.reshape(1, bs_, ts, d))
```

```python
import functools
import math

import numpy as np
import jax
import jax.numpy as jnp
from jax import lax
from jax.experimental import pallas as pl
from jax.experimental.pallas import tpu as pltpu

F32 = jnp.float32
BF16 = jnp.bfloat16

EPS = 1e-6
CHUNK = 64
GMLP_CHUNK = 128
GMLP_GROUPS = 8
N_MODS = 9
NEG_BIG = -1e30

V7X_VMEM_BYTES = 64 * 1024 * 1024
VMEM_LIMIT_BYTES = V7X_VMEM_BYTES - 8 * 1024 * 1024
LANES = 128
SUBLANES = 8
BF16_SUBLANES = 16
MXU_DIM = 256
ATTN_SUB = 2 * MXU_DIM
ATTN_TQ = 1024
FFN_CHUNK = 256
ALIBI_SPLIT = 256
ALIBI_PERIOD = 1024


def _lambda_init(layer_idx):
    return 0.8 - 0.6 * math.exp(-0.3 * layer_idx)


def _const_spec(shape):
    nd = len(shape)
    return pl.BlockSpec(shape, lambda *_: (0,) * nd, pipeline_mode=pl.Buffered(1))


def _params(n_axes):
    return pltpu.CompilerParams(
        dimension_semantics=("arbitrary",) * n_axes, vmem_limit_bytes=VMEM_LIMIT_BYTES)


def _mods_kernel(c_ref, w_ref, b_ref, o_ref):
    c = c_ref[...]
    h = c * jax.nn.sigmoid(c)
    o_ref[...] = jnp.dot(h, w_ref[...], preferred_element_type=F32,
                         precision=lax.Precision.HIGHEST) + b_ref[...]


def _mods(c_all, ada_w, ada_b):
    n, d = c_all.shape
    width = ada_w.shape[1]
    tn = d
    return pl.pallas_call(
        _mods_kernel,
        grid=(width // tn,),
        in_specs=[pl.BlockSpec((n, d), lambda j: (0, 0)),
                  pl.BlockSpec((d, tn), lambda j: (0, j)),
                  pl.BlockSpec((1, tn), lambda j: (0, j))],
        out_specs=pl.BlockSpec((n, tn), lambda j: (0, j)),
        out_shape=jax.ShapeDtypeStruct((n, width), F32),
        compiler_params=_params(1),
        name="adaln_mods",
    )(c_all, ada_w, ada_b.reshape(1, width))


def _rms(x):
    return x * lax.rsqrt(jnp.mean(x * x, axis=-1, keepdims=True) + EPS)


def _mod_norm(x, g, mods_ref, first):
    return (_rms(x) * g) * (1.0 + mods_ref[0, first + 1]) + mods_ref[0, first]


def _swiglu(h, wgu_ref, wd_ref):
    f = wd_ref.shape[0]
    acc = None
    for c0 in range(0, f, FFN_CHUNK):
        gate = jnp.dot(h, wgu_ref[:, c0:c0 + FFN_CHUNK], preferred_element_type=F32)
        up = jnp.dot(h, wgu_ref[:, f + c0:f + c0 + FFN_CHUNK], preferred_element_type=F32)
        a = (gate * jax.nn.sigmoid(gate) * up).astype(BF16)
        part = jnp.dot(a, wd_ref[c0:c0 + FFN_CHUNK, :], preferred_element_type=F32)
        acc = part if acc is None else acc + part
    return acc


def _gelu_tanh(x):
    return 0.5 * x * (1.0 + jnp.tanh(math.sqrt(2.0 / math.pi) * (x + 0.044715 * (x * x * x))))


def _row_specs(x, mods, tm):
    _, _, d = x.shape
    r = mods.shape[2]
    row_spec = pl.BlockSpec((1, tm, d), lambda g, i: (g, i, 0))
    if r == 1:
        mods_spec = pl.BlockSpec((1, N_MODS, 1, d), lambda g, i: (g, 0, 0, 0))
    else:
        mods_spec = pl.BlockSpec((1, N_MODS, tm, d), lambda g, i: (g, 0, i, 0))
    return row_spec, mods_spec


def _ffn_kernel(x_ref, mods_ref, ng_ref, wgu_ref, wd_ref, o_ref):
    x = x_ref[0]
    h = _mod_norm(x, ng_ref[0:1, :], mods_ref, 0).astype(BF16)
    o_ref[0] = x + 0.5 * mods_ref[0, 2] * _swiglu(h, wgu_ref, wd_ref)


def _ffn1(x, mods, norm_g, wgu, wd, tm):
    g, t, d = x.shape
    row_spec, mods_spec = _row_specs(x, mods, tm)
    return pl.pallas_call(
        _ffn_kernel,
        grid=(g, t // tm),
        in_specs=[row_spec, mods_spec,
                  _const_spec(norm_g.shape), _const_spec(wgu.shape), _const_spec(wd.shape)],
        out_specs=row_spec,
        out_shape=jax.ShapeDtypeStruct((g, t, d), F32),
        compiler_params=_params(2),
        name="ffn1",
    )(x, mods, norm_g, wgu, wd)


def _proj_kernel(x_ref, mods_ref, ng_ref, win_ref, wgate_ref, bgate_ref, qg_ref, kg_ref, gsum_ref,
                 gbcast_ref, vng_ref, ws_ref, bs_ref, wbb_ref,
                 q_o, k_o, v_o, kb_o, vb_o, ga_o, gb_o, *maybe_gv_o, rows, q_scale):
    x = x_ref[0]
    tm, d = x.shape
    h = _mod_norm(x, ng_ref[1:2, :], mods_ref, 3).astype(BF16)

    def w_in(j):
        return win_ref[:, j * d:(j + 1) * d]

    def group_norm(z, g):
        ms = jnp.dot((z * z).astype(BF16), gsum_ref[...], preferred_element_type=F32)
        r = lax.rsqrt(ms + EPS)
        r_hi = r.astype(BF16)
        r_lo = (r - r_hi.astype(F32)).astype(BF16)
        scale = jnp.dot(jnp.concatenate([r_hi, r_lo], axis=1), gbcast_ref[...],
                        preferred_element_type=F32)
        return z * scale * g

    q = group_norm(jnp.dot(h, w_in(0), preferred_element_type=F32), qg_ref[...])
    q_o[0] = (q * q_scale).astype(BF16)
    k = group_norm(jnp.dot(h, w_in(1), preferred_element_type=F32), kg_ref[...])
    k_o[0] = k
    kb_o[0] = k.astype(BF16)
    v = jnp.dot(h, w_in(2), preferred_element_type=F32)
    v_o[0] = v
    vb_o[0] = v.astype(BF16)

    u = _gelu_tanh(jnp.dot(h, w_in(3), preferred_element_type=F32))
    gz = _gelu_tanh(jnp.dot(h, w_in(4), preferred_element_type=F32))
    gv = _rms(gz) * vng_ref[...]
    if maybe_gv_o:
        maybe_gv_o[0][0] = gv
    gvb = gv.astype(BF16)

    row_i = lax.broadcasted_iota(jnp.int32, (rows, rows), 0)
    col_i = lax.broadcasted_iota(jnp.int32, (rows, rows), 1)
    tril = col_i <= row_i
    gw = d // GMLP_GROUPS
    ws = [jnp.where(tril, ws_ref[g], 0.0).astype(BF16) for g in range(GMLP_GROUPS)]
    s_chunks = []
    for c in range(tm // rows):
        r0 = c * rows
        mix = jnp.concatenate(
            [jnp.dot(ws[g], gvb[r0:r0 + rows, g * gw:(g + 1) * gw], preferred_element_type=F32)
             for g in range(GMLP_GROUPS)], axis=1)
        s_chunks.append(u[r0:r0 + rows, :] * (mix + bs_ref[...]))
    s = s_chunks[0] if len(s_chunks) == 1 else jnp.concatenate(s_chunks, axis=0)
    sb = jnp.dot(s.astype(BF16), wbb_ref[...], preferred_element_type=F32)

    gate_a = jax.nn.sigmoid(
        jnp.dot(h, wgate_ref[:, 0:d], preferred_element_type=F32) + bgate_ref[0:1, :])
    gate_b = jax.nn.sigmoid(
        jnp.dot(h, wgate_ref[:, d:2 * d], preferred_element_type=F32) + bgate_ref[1:2, :])
    ga_o[0] = gate_a.astype(BF16)
    gb_o[0] = (gate_b * sb).astype(BF16)


def _proj(x, mods, norm_g, win, wgate, bgate, qg, kg, gsum, gbcast, vng, ws, bs, wbb, tm, rows,
          q_scale, want_gv):
    g, t, d = x.shape
    row_spec, mods_spec = _row_specs(x, mods, tm)
    out_shapes = [jax.ShapeDtypeStruct((g, t, d), BF16),
                  jax.ShapeDtypeStruct((g, t, d), F32),
                  jax.ShapeDtypeStruct((g, t, d), F32),
                  jax.ShapeDtypeStruct((g, t, d), BF16),
                  jax.ShapeDtypeStruct((g, t, d), BF16),
                  jax.ShapeDtypeStruct((g, t, d), BF16),
                  jax.ShapeDtypeStruct((g, t, d), BF16)]
    if want_gv:
        out_shapes.append(jax.ShapeDtypeStruct((g, t, d), F32))
    consts = (norm_g, win, wgate, bgate, qg, kg, gsum, gbcast, vng, ws, bs, wbb)
    return pl.pallas_call(
        functools.partial(_proj_kernel, rows=rows, q_scale=q_scale),
        grid=(g, t // tm),
        in_specs=[row_spec, mods_spec] + [_const_spec(a.shape) for a in consts],
        out_specs=[row_spec] * len(out_shapes),
        out_shape=out_shapes,
        compiler_params=_params(2),
        name="mixer_proj",
    )(x, mods, *consts)


def _split_maps(q):
    lane = lax.broadcasted_iota(jnp.int32, q.shape, 1)
    half = q.shape[1] // 2
    zero = jnp.zeros_like(q)
    return jnp.where(lane < half, q, zero), jnp.where(lane >= half, q, zero)


def _qk(a, k):
    return lax.dot_general(a, k, (((1,), (1,)), ((), ())), preferred_element_type=F32)


def _lambda(lq_ref, lam_init):
    lq = lq_ref[...]
    t1 = jnp.sum(lq[0:1, :] * lq[1:2, :], axis=-1, keepdims=True)
    t2 = jnp.sum(lq[2:3, :] * lq[3:4, :], axis=-1, keepdims=True)
    return jnp.exp(t1) - jnp.exp(t2) + lam_init


def _finish_heads(acc1, l1, acc2, l2, lam, sg, lam_init):
    o = acc1 / l1 - lam * (acc2 / l2)
    return (_rms(o) * sg) * (1.0 - lam_init)


def _attn_kernel(slope_ref, tile_ref, period_ref, bias_ref, first_ref, last_ref, dist_ref,
                 q_ref, k_ref, v_ref, eqt_ref, ek_ref, lq_ref, sg_ref, o_ref,
                 kx_scr, vt_scr, bias_scr, at_scr, s_scr, p_last_scr, smax_scr, m_scr, fin_scr,
                 acc_scr, *, lam_init, wide, tq, n_steps):
    h = pl.program_id(0)
    t_all = k_ref.shape[1]
    slope = slope_ref[h]
    per_wide = wide // tq

    @pl.when(pl.program_id(1) == 0)
    def _():
        for c in range(t_all // wide):
            kx_scr[c * wide:(c + 1) * wide, LANES:2 * LANES] = ek_ref[0]
        vt_scr[LANES:, :] = jnp.ones((vt_scr.shape[0] - LANES, t_all), BF16)
        s_pos = lax.broadcasted_iota(jnp.int32, (wide, tq), 0)
        for c in range(per_wide):
            t_pos = c * tq + lax.broadcasted_iota(jnp.int32, (wide, tq), 1)
            visible = (s_pos // CHUNK) <= (t_pos // CHUNK)
            fix = (jnp.abs(t_pos - s_pos) - (t_pos - s_pos)).astype(F32)
            bias_scr[c] = jnp.where(visible, -slope * fix, NEG_BIG)

    kx_scr[:, 0:LANES] = k_ref[0]
    p_last_scr[...] = jnp.zeros(p_last_scr.shape, BF16)
    acc_scr[...] = jnp.zeros(acc_scr.shape, F32)
    m_scr[...] = jnp.full(m_scr.shape, NEG_BIG, F32)

    def build_queries(tile):
        r0 = pl.multiple_of(tile * tq, tq)
        qt = q_ref[0, pl.ds(r0, tq), :].astype(F32).T
        row = lax.broadcasted_iota(jnp.int32, qt.shape, 0)
        zero = jnp.zeros_like(qt)
        at_scr[0, 0:LANES, :] = jnp.where(row < LANES // 2, qt, zero).astype(BF16)
        at_scr[1, 0:LANES, :] = jnp.where(row >= LANES // 2, qt, zero).astype(BF16)
        alibi_rows = eqt_ref[0, tile % per_wide]
        at_scr[0, LANES:2 * LANES, :] = alibi_rows
        at_scr[1, LANES:2 * LANES, :] = alibi_rows

    sub = p_last_scr.shape[1]
    n_sub = wide // sub

    def scores(e, slot, masked):
        kx = kx_scr[pl.ds(pl.multiple_of(period_ref[e] * wide, wide), wide), :]
        for mp in range(2):
            s = jnp.dot(kx, at_scr[mp], preferred_element_type=F32)
            if masked:
                s = s + bias_scr[bias_ref[e]]
            s_scr[slot, mp] = s
            smax_scr[slot, mp, 0:1, :] = jnp.max(s, axis=0, keepdims=True)

    def values_block(e, c):
        off = pl.multiple_of(period_ref[e] * wide + c * sub, sub)
        return vt_scr[:, pl.ds(off, sub)]

    def pipeline_step(e, slot, masked, final=False):
        shift = -slope * (dist_ref[e] * wide).astype(F32)
        is_first = first_ref[e] == 1
        offsets, alphas = [], []
        for mp in range(2):
            m_old = jnp.where(is_first, NEG_BIG, m_scr[mp, 0:1, :])
            m_new = jnp.maximum(m_old, smax_scr[slot, mp, 0:1, :] + shift)
            alphas.append(jnp.exp2(m_old - m_new))
            m_scr[mp, 0:1, :] = m_new
            offsets.append(m_new - shift)
        vt_pending = values_block(e - 1, n_sub - 1)
        pending = [jnp.dot(vt_pending, p_last_scr[mp], preferred_element_type=F32) for mp in range(2)]
        if not final:
            scores(e + 1, 1 - slot, masked)
        pv = [None, None]
        for c in range(n_sub):
            rows = slice(c * sub, (c + 1) * sub)
            for mp in range(2):
                p = jnp.exp2(s_scr[slot, mp, rows, :] - offsets[mp]).astype(BF16)
                if c == n_sub - 1 and not final:
                    p_last_scr[mp] = p
                else:
                    part = jnp.dot(values_block(e, c), p, preferred_element_type=F32)
                    pv[mp] = part if pv[mp] is None else pv[mp] + part
        for mp in range(2):
            done = acc_scr[mp] + pending[mp]
            fin_scr[mp] = done
            acc_scr[mp] = alphas[mp] * done + pv[mp]

    def finalize(tile, src=fin_scr):
        lam = _lambda(lq_ref, lam_init)
        a1 = src[0]
        a2 = src[1]
        inv1 = 1.0 / a1[LANES:LANES + 1]
        inv2 = lam / a2[LANES:LANES + 1]
        ot = a1[0:LANES] * inv1 - a2[0:LANES] * inv2
        ot = ot * lax.rsqrt(jnp.mean(ot * ot, axis=0, keepdims=True) + EPS)
        r0 = pl.multiple_of(tile * tq, tq)
        o_ref[0, pl.ds(r0, tq), :] = ((ot.T * sg_ref[...]) * (1.0 - lam_init)).astype(o_ref.dtype)

    @pl.when(first_ref[1] == 1)
    def _():
        build_queries(tile_ref[1])

    @pl.when(first_ref[1] == 1)
    def _():
        scores(1, 0, True)
        for c in range(t_all // wide):
            vt_scr[0:LANES, c * wide:(c + 1) * wide] = (
                v_ref[0, c * wide:(c + 1) * wide, :].astype(F32).T.astype(BF16))

    def iteration(k, carry):
        e = k + 1

        @pl.when(first_ref[e + 1] == 1)
        def _():
            build_queries(tile_ref[e + 1])

        next_masked = bias_ref[e + 1] < per_wide
        for slot in (0, 1):
            for masked in (False, True):
                cond = next_masked if masked else jnp.logical_not(next_masked)

                @pl.when(jnp.logical_and(k % 2 == slot, cond))
                def _():
                    pipeline_step(e, slot, masked)

        @pl.when(last_ref[e - 1] == 1)
        def _():
            finalize(tile_ref[e - 1])

        return carry

    lax.fori_loop(0, n_steps - 1, iteration, 0)

    e_last = n_steps

    @pl.when(first_ref[1] == 1)
    def _():
        pipeline_step(e_last, (n_steps - 1) % 2, False, final=True)

    @pl.when(last_ref[e_last - 1] == 1)
    def _():
        finalize(tile_ref[e_last - 1])

    @pl.when(first_ref[1] == 1)
    def _():
        finalize(tile_ref[e_last], acc_scr)


def _bf16_pieces(x, n):
    pieces, rest = [], np.asarray(x, np.float32)
    for _ in range(n):
        p = rest.astype(BF16).astype(np.float32)
        pieces.append(p)
        rest = rest - p
    return pieces


def _alibi_rows_cols(slopes, period):
    n_heads = slopes.shape[0]
    pos = np.arange(period)
    offs = ((pos // ALIBI_SPLIT) * ALIBI_SPLIT, pos % ALIBI_SPLIT)
    eqt = np.zeros((n_heads, LANES, period), np.float32)
    ek = np.zeros((n_heads, period, LANES), np.float32)
    lane = 0
    for piece in _bf16_pieces(slopes, 3):
        for off in offs:
            eqt[:, lane, :] = off[None, :]
            ek[:, :, lane] = -piece[:, None]
            eqt[:, lane + 1, :] = piece[:, None]
            ek[:, :, lane + 1] = off[None, :]
            lane += 2
    return jnp.asarray(eqt, BF16), jnp.asarray(ek, BF16)


def _attention_schedule(n_tiles, per_wide):
    tile, period, bias, first, last, dist = [], [], [], [], [], []
    for i in range(n_tiles):
        n_wide = i // per_wide
        for j in range(n_wide + 1):
            own = j == n_wide
            tile.append(i)
            period.append(j)
            bias.append(i % per_wide if own else per_wide)
            first.append(int(j == 0))
            last.append(int(own))
            dist.append(n_wide - j)
    pad = lambda v, fill: np.asarray([fill] + v + [fill, fill], np.int32)
    return (pad(tile, n_tiles - 1), pad(period, 0), pad(bias, per_wide), pad(first, 1),
            pad(last, 0), pad(dist, 0)), len(tile)


def _attention(q, kb, vb, slopes, lq, sg, lam_init):
    b, t, width = q.shape
    n_heads = width // LANES
    tq = min(ATTN_TQ, t)
    wide = min(ALIBI_PERIOD, t)
    assert t % wide == 0 and wide % tq == 0
    per_wide = wide // tq
    eqt, ek = _alibi_rows_cols(slopes, wide)
    eqt = eqt.reshape(n_heads, LANES, per_wide, tq).transpose(0, 2, 1, 3)
    tables, n_steps = _attention_schedule(t // tq, per_wide)
    ones_rows = BF16_SUBLANES
    n_prefetch = 1 + len(tables)
    head_spec = pl.BlockSpec((1, t, LANES), lambda h, bi, *_: (bi, 0, h))
    grid_spec = pltpu.PrefetchScalarGridSpec(
        num_scalar_prefetch=n_prefetch,
        grid=(n_heads, b),
        in_specs=[head_spec, head_spec, head_spec,
                  pl.BlockSpec((1, per_wide, LANES, tq), lambda h, bi, *_: (h, 0, 0, 0)),
                  pl.BlockSpec((1, wide, LANES), lambda h, bi, *_: (h, 0, 0)),
                  pl.BlockSpec(lq.shape, lambda h, bi, *_: (0, 0)),
                  pl.BlockSpec(sg.shape, lambda h, bi, *_: (0, 0))],
        out_specs=head_spec,
        scratch_shapes=[pltpu.VMEM((t, 2 * LANES), BF16),
                        pltpu.VMEM((LANES + ones_rows, t), BF16),
                        pltpu.VMEM((per_wide, wide, tq), F32),
                        pltpu.VMEM((2, 2 * LANES, tq), BF16),
                        pltpu.VMEM((2, 2, wide, tq), F32),
                        pltpu.VMEM((2, ATTN_SUB, tq), BF16),
                        pltpu.VMEM((2, 2, SUBLANES, tq), F32),
                        pltpu.VMEM((2, SUBLANES, tq), F32),
                        pltpu.VMEM((2, LANES + ones_rows, tq), F32),
                        pltpu.VMEM((2, LANES + ones_rows, tq), F32)],
    )
    return pl.pallas_call(
        functools.partial(_attn_kernel, lam_init=lam_init, wide=wide, tq=tq, n_steps=n_steps),
        grid_spec=grid_spec,
        out_shape=jax.ShapeDtypeStruct((b, t, width), BF16),
        compiler_params=_params(2),
        name="diff_attention",
    )(jnp.asarray(slopes, F32), *tables, q, kb, vb, eqt, ek, lq, sg)


def _attn_cache_kernel(q_ref, kn_ref, vn_ref, ck_ref, cv_ref, lq_ref, sg_ref, o_ref, *,
                       lam_init, slopes):
    tn = q_ref.shape[1]
    past = ck_ref.shape[1]
    lam = _lambda(lq_ref, lam_init)

    def distance(k0, width):
        qpos = past + lax.broadcasted_iota(jnp.int32, (tn, width), 0)
        kpos = k0 + lax.broadcasted_iota(jnp.int32, (tn, width), 1)
        visible = (kpos // CHUNK) <= (qpos // CHUNK)
        return visible, jnp.abs(qpos - kpos).astype(F32)

    vis_c, dist_c = distance(0, past)
    vis_n, dist_n = distance(past, tn)

    for h, slope in enumerate(slopes):
        cols = slice(h * LANES, (h + 1) * LANES)
        q1, q2 = _split_maps(q_ref[0, :, cols])
        ck = ck_ref[0, :, cols].astype(BF16)
        cv = cv_ref[0, :, cols].astype(BF16)
        kn = kn_ref[0, :, cols]
        vn = vn_ref[0, :, cols]
        bias_c = jnp.where(vis_c, -slope * dist_c, NEG_BIG)
        bias_n = jnp.where(vis_n, -slope * dist_n, NEG_BIG)

        def one_map(a):
            sc = _qk(a, ck) + bias_c
            sn = _qk(a, kn) + bias_n
            m = jnp.maximum(jnp.max(sc, axis=-1, keepdims=True), jnp.max(sn, axis=-1, keepdims=True))
            pc = jnp.exp2(sc - m)
            pn = jnp.exp2(sn - m)
            l = jnp.sum(pc, axis=-1, keepdims=True) + jnp.sum(pn, axis=-1, keepdims=True)
            acc = (jnp.dot(pc.astype(BF16), cv, preferred_element_type=F32)
                   + jnp.dot(pn.astype(BF16), vn, preferred_element_type=F32))
            return l, acc

        l1, acc1 = one_map(q1)
        l2, acc2 = one_map(q2)
        o_ref[0, :, cols] = _finish_heads(acc1, l1, acc2, l2, lam, sg_ref[...],
                                          lam_init).astype(o_ref.dtype)


def _attention_cached(q, kb, vb, cache_k, cache_v, slopes, lq, sg, lam_init):
    b, tn, width = q.shape
    past = cache_k.shape[1]
    new_spec = pl.BlockSpec((1, tn, width), lambda bi: (bi, 0, 0))
    past_spec = pl.BlockSpec((1, past, width), lambda bi: (bi, 0, 0))
    return pl.pallas_call(
        functools.partial(_attn_cache_kernel, lam_init=lam_init,
                          slopes=tuple(float(s) for s in slopes)),
        grid=(b,),
        in_specs=[new_spec, new_spec, new_spec, past_spec, past_spec,
                  pl.BlockSpec(lq.shape, lambda bi: (0, 0)),
                  pl.BlockSpec(sg.shape, lambda bi: (0, 0))],
        out_specs=new_spec,
        out_shape=jax.ShapeDtypeStruct((b, tn, width), BF16),
        compiler_params=_params(1),
        name="diff_attention_cached",
    )(q, kb, vb, cache_k, cache_v, lq, sg)


def _out_kernel(x_ref, o_ref, ga_ref, gb_ref, mods_ref, ng_ref, wba_ref, wout_ref, wgu_ref, wd_ref,
                y_ref):
    x = x_ref[0]
    oa = jnp.dot(o_ref[0], wba_ref[...], preferred_element_type=F32)
    mixed = (ga_ref[0].astype(F32) * oa + gb_ref[0].astype(F32)).astype(BF16)
    x = x + mods_ref[0, 5] * jnp.dot(mixed, wout_ref[...], preferred_element_type=F32)
    h = _mod_norm(x, ng_ref[2:3, :], mods_ref, 6).astype(BF16)
    x = x + 0.5 * mods_ref[0, 8] * _swiglu(h, wgu_ref, wd_ref)
    y_ref[0] = _rms(x) * ng_ref[3:4, :]


def _mix_ffn2(x, o, ga, gb, mods, norm_g, wba, wout, wgu, wd, tm):
    g, t, d = x.shape
    row_spec, mods_spec = _row_specs(x, mods, tm)
    consts = (norm_g, wba, wout, wgu, wd)
    return pl.pallas_call(
        _out_kernel,
        grid=(g, t // tm),
        in_specs=[row_spec] * 4 + [mods_spec] + [_const_spec(a.shape) for a in consts],
        out_specs=row_spec,
        out_shape=jax.ShapeDtypeStruct((g, t, d), F32),
        compiler_params=_params(2),
        name="mix_ffn2",
    )(x, o, ga, gb, mods, *consts)


def _row_tile(t, target):
    tm = min(t, target)
    while t % tm:
        tm //= 2
    return tm


def kernel(x_prompt, x_sample, cache_k, cache_v, c_prompt, c_sample, ada_w, ada_b, norm_g, ffn1_wgu,
           ffn1_wd, w_in, q_norm_g, k_norm_g, lambda_qk, attn_subln_g, gmlp_vnorm_g, gmlp_ws, gmlp_bs,
           w_gate, b_gate, w_branch, w_out, ffn2_wgu, ffn2_wd):
    assert ada_w.shape[0] == 1, "single-layer step"
    bp, tp, d = x_prompt.shape
    bs_, ts, _ = x_sample.shape
    _, _, past, n_heads, _, hd = cache_k.shape
    assert n_heads == 8 and 2 * hd == LANES and n_heads * LANES == d
    assert gmlp_ws.shape[1:] == (GMLP_GROUPS, GMLP_CHUNK, GMLP_CHUNK)
    lam_init = _lambda_init(0)
    log2e = math.log2(math.e)
    slopes = (2.0 ** (-8.0 * np.arange(1, n_heads + 1) / n_heads) * log2e).astype(np.float32)
    q_scale = float(hd) ** -0.5 * log2e

    assert ffn1_wd.shape[1] % FFN_CHUNK == 0 and w_in.shape[2] == 5 * d
    wgu1, wd1 = ffn1_wgu[0].astype(BF16), ffn1_wd[0].astype(BF16)
    wgu2, wd2 = ffn2_wgu[0].astype(BF16), ffn2_wd[0].astype(BF16)
    win = w_in[0].astype(BF16)
    wgate = w_gate[0].astype(BF16)
    bgate = b_gate[0].reshape(2, d)
    wba = w_branch[0, :d].astype(BF16)
    wbb = w_branch[0, d:].astype(BF16)
    wout = w_out[0].astype(BF16)
    ng = norm_g[0]
    qg = jnp.tile(q_norm_g[0], d // hd).reshape(1, d)
    kg = jnp.tile(k_norm_g[0], d // hd).reshape(1, d)
    n_groups = d // hd
    member = (np.arange(d)[:, None] // hd == np.arange(LANES)[None, :]).astype(np.float32)
    gsum = jnp.asarray(member / hd, BF16)
    gbcast = jnp.asarray(np.concatenate([member.T, member.T], axis=0), BF16)
    assert n_groups <= LANES
    vng = gmlp_vnorm_g[0].reshape(1, d)
    sg = attn_subln_g[0].reshape(1, LANES)
    lq = lambda_qk[0]

    mods = _mods(jnp.concatenate([c_prompt, c_sample], axis=0), ada_w[0], ada_b[0])
    mods = mods.reshape(bp + bs_, N_MODS, d)
    mods_p = mods[:bp].reshape(bp, N_MODS, 1, d)
    mods_s = jnp.repeat(mods[bp:].transpose(1, 0, 2), ts, axis=1).reshape(1, N_MODS, bs_ * ts, d)

    def gmlp_consts(t):
        rows = min(t, GMLP_CHUNK)
        ws = gmlp_ws[0][:, :rows, :rows]
        bs = jnp.repeat(gmlp_bs[0][:, :rows].T, d // GMLP_GROUPS, axis=1)
        return rows, ws, bs

    def layer(x, mods_x, seq, attend, tm_ffn, tm_proj, want_gv):
        _, t, _ = x.shape
        rows, ws, bs = gmlp_consts(seq)
        x1 = _ffn1(x, mods_x, ng, wgu1, wd1, _row_tile(t, tm_ffn))
        tmp = max(_row_tile(t, tm_proj), rows)
        outs = _proj(x1, mods_x, ng, win, wgate, bgate, qg, kg, gsum, gbcast, vng, ws, bs, wbb,
                     tmp, rows, q_scale, want_gv)
        q, k, v, kb, vb, ga, gb = outs[:7]
        o = attend(q, kb, vb)
        y = _mix_ffn2(x1, o, ga, gb, mods_x, ng, wba, wout, wgu2, wd2, _row_tile(t, tm_ffn))
        return y, k, v, (outs[7] if want_gv else None)

    y_p, k_p, v_p, _ = layer(
        x_prompt, mods_p, tp, lambda q, kb, vb: _attention(q, kb, vb, slopes, lq, sg, lam_init),
        512, 512, False)

    ck = cache_k[0].reshape(bs_, past, d)
    cv = cache_v[0].reshape(bs_, past, d)

    def attend_sample(q, kb, vb):
        shape = (bs_, ts, d)
        o = _attention_cached(q.reshape(shape), kb.reshape(shape), vb.reshape(shape), ck, cv,
                              slopes, lq, sg, lam_init)
        return o.reshape(1, bs_ * ts, d)

    y_s, k_s, v_s, gv_s = layer(
        x_sample.reshape(1, bs_ * ts, d), mods_s, ts, attend_sample, 512, 512, True)

    return (y_p, y_s.reshape(bs_, ts, d),
            k_p.reshape(1, bp, tp, n_heads, 2, hd), v_p.reshape(1, bp, tp, n_heads, 2 * hd),
            k_s.reshape(1, bs_, ts, n_heads, 2, hd), v_s.reshape(1, bs_, ts, n_heads, 2 * hd),
            gv_s.reshape(1, bs_, ts, d))
```
